```python
import math
import jax, jax.numpy as jnp
from jax import lax
import numpy as np

D_MODEL = 2048
BATCH = 4
SEQ = 4096
DEPTH = 4

N_MIXERS = 3
Q_BLOCK = 128
NEG_INF = -1e30
ROPE_THETA = 10000.0
LN_EPS = 1e-5
RMS_EPS = 1e-6
DEEPNORM_ALPHA = (2 * DEPTH) ** 0.25
DEEPNORM_BETA = (8 * DEPTH) ** -0.25

MLA_HEADS = D_MODEL // 128
MLA_NOPE_DIM = 128
MLA_ROPE_DIM = 64
MLA_V_DIM = 128
MLA_Q_RANK = D_MODEL // 4
MLA_KV_RANK = D_MODEL // 4

DIFF_HEAD_DIM = 128
DIFF_HEADS = D_MODEL // (2 * DIFF_HEAD_DIM)

SB_HEAD_DIM = 128
SB_HEADS = D_MODEL // SB_HEAD_DIM

N_EXPERTS = 32
TOP_K = 4
D_EXPERT = 3 * D_MODEL // 8
SWIGLU_LIMIT = 7.0
SWIGLU_ALPHA = 1.702
MOE_BLOCK = 128

N_MLA_LAYERS = (DEPTH + 2) // 3
N_DIFF_LAYERS = (DEPTH + 1) // 3
N_SB_LAYERS = DEPTH // 3

kernel_name = "hybrid_mla_diff_stickbreak_moe_deepnorm_adaln"


def _layer_norm(x, g, b):
    xf = x.astype(jnp.float32)
    mu = jnp.mean(xf, axis=-1, keepdims=True)
    var = jnp.mean(jnp.square(xf - mu), axis=-1, keepdims=True)
    return ((xf - mu) * lax.rsqrt(var + LN_EPS) * g + b).astype(x.dtype)


def _rms_norm(x, g, eps):
    xf = x.astype(jnp.float32)
    return (xf * lax.rsqrt(jnp.mean(xf * xf, axis=-1, keepdims=True) + eps) * g).astype(x.dtype)


def _rope(t, pos):
    d = t.shape[-1]
    half = d // 2
    inv_freq = ROPE_THETA ** (-jnp.arange(half, dtype=jnp.float32) * (2.0 / d))
    ang = pos.astype(jnp.float32)[:, None] * inv_freq[None, :]
    bshape = (1, pos.shape[0]) + (1,) * (t.ndim - 3) + (half,)
    cos = jnp.cos(ang).reshape(bshape)
    sin = jnp.sin(ang).reshape(bshape)
    tf = t.astype(jnp.float32)
    t1, t2 = tf[..., :half], tf[..., half:]
    return jnp.concatenate([t1 * cos - t2 * sin, t2 * cos + t1 * sin], axis=-1).astype(t.dtype)


def _to_blocks(t):
    b, s = t.shape[:2]
    return jnp.swapaxes(t.reshape((b, s // Q_BLOCK, Q_BLOCK) + t.shape[2:]), 0, 1)


def _from_blocks(t):
    t = jnp.swapaxes(t, 0, 1)
    return t.reshape((t.shape[0], t.shape[1] * t.shape[2]) + t.shape[3:])


def _sweep_query_blocks(block_fn, *q_parts):
    s = q_parts[0].shape[1]
    q_pos = jnp.arange(s, dtype=jnp.int32).reshape(s // Q_BLOCK, Q_BLOCK)
    out = lax.map(lambda args: block_fn(*args), (q_pos,) + tuple(_to_blocks(t) for t in q_parts))
    return _from_blocks(out)


def _mla(h, pos, w_in, q_norm_g, kv_norm_g, w_uq, w_ukv, w_o):
    b, s, _ = h.shape
    lat = h @ w_in
    c_q = _rms_norm(lat[..., :MLA_Q_RANK], q_norm_g, RMS_EPS)
    c_kv = _rms_norm(lat[..., MLA_Q_RANK:MLA_Q_RANK + MLA_KV_RANK], kv_norm_g, RMS_EPS)
    k_rope = _rope(lat[..., MLA_Q_RANK + MLA_KV_RANK:], pos)
    q = (c_q @ w_uq).reshape(b, s, MLA_HEADS, MLA_NOPE_DIM + MLA_ROPE_DIM)
    q_nope = q[..., :MLA_NOPE_DIM]
    q_rope = _rope(q[..., MLA_NOPE_DIM:], pos)
    kv = (c_kv @ w_ukv).reshape(b, s, MLA_HEADS, MLA_NOPE_DIM + MLA_V_DIM)
    k_nope = kv[..., :MLA_NOPE_DIM]
    v = kv[..., MLA_NOPE_DIM:]
    scale = (MLA_NOPE_DIM + MLA_ROPE_DIM) ** -0.5
    k_pos = jnp.arange(s, dtype=jnp.int32)

    def block(q_pos, qn, qr):
        sc = (jnp.einsum('bqhd,bkhd->bhqk', qn, k_nope)
              + jnp.einsum('bqhr,bkr->bhqk', qr, k_rope)).astype(jnp.float32) * scale
        sc = jnp.where(k_pos[None, :] <= q_pos[:, None], sc, NEG_INF)
        p = jax.nn.softmax(sc, axis=-1).astype(v.dtype)
        return jnp.einsum('bhqk,bkhd->bqhd', p, v)

    o = _sweep_query_blocks(block, q_nope, q_rope)
    return o.reshape(b, s, MLA_HEADS * MLA_V_DIM) @ w_o


def _diff_attention(h, pos, layer_idx, w_qkv, lam, subln_g, w_o):
    b, s, _ = h.shape
    nh, hd = DIFF_HEADS, DIFF_HEAD_DIM
    width = nh * 2 * hd
    qkv = h @ w_qkv
    q = _rope(qkv[..., :width].reshape(b, s, nh, 2, hd), pos)
    k = _rope(qkv[..., width:2 * width].reshape(b, s, nh, 2, hd), pos)
    v = qkv[..., 2 * width:].reshape(b, s, nh, 2 * hd)
    lam_init = 0.8 - 0.6 * math.exp(-0.3 * layer_idx)
    lf = lam.astype(jnp.float32)
    lam_full = jnp.exp(jnp.sum(lf[0] * lf[1])) - jnp.exp(jnp.sum(lf[2] * lf[3])) + lam_init
    k1, k2 = k[..., 0, :], k[..., 1, :]
    scale = hd ** -0.5
    k_pos = jnp.arange(s, dtype=jnp.int32)

    def block(q_pos, q1, q2):
        mask = k_pos[None, :] <= q_pos[:, None]
        s1 = jnp.where(mask, jnp.einsum('bqhd,bkhd->bhqk', q1, k1).astype(jnp.float32) * scale, NEG_INF)
        s2 = jnp.where(mask, jnp.einsum('bqhd,bkhd->bhqk', q2, k2).astype(jnp.float32) * scale, NEG_INF)
        a = jax.nn.softmax(s1, axis=-1) - lam_full * jax.nn.softmax(s2, axis=-1)
        return jnp.einsum('bhqk,bkhd->bqhd', a.astype(v.dtype), v)

    o = _sweep_query_blocks(block, q[..., 0, :], q[..., 1, :])
    o = _rms_norm(o, subln_g, LN_EPS) * (1.0 - lam_init)
    return o.reshape(b, s, width) @ w_o


def _stick_breaking(h, w_qkv, w_o):
    b, s, _ = h.shape
    width = SB_HEADS * SB_HEAD_DIM
    qkv = h @ w_qkv
    q = qkv[..., :width].reshape(b, s, SB_HEADS, SB_HEAD_DIM)
    k = qkv[..., width:2 * width].reshape(b, s, SB_HEADS, SB_HEAD_DIM)
    v = qkv[..., 2 * width:].reshape(b, s, SB_HEADS, SB_HEAD_DIM)
    scale = SB_HEAD_DIM ** -0.5
    k_pos = jnp.arange(s, dtype=jnp.int32)

    def block(q_pos, qb):
        z = jnp.einsum('bqhd,bkhd->bhqk', qb, k).astype(jnp.float32) * scale
        strict = k_pos[None, :] < q_pos[:, None]
        log_keep = jnp.where(strict, jax.nn.log_sigmoid(-z), 0.0)
        later = lax.cumsum(log_keep, axis=3, reverse=True) - log_keep
        a = jnp.where(strict, jnp.exp(jax.nn.log_sigmoid(z) + later), 0.0)
        return jnp.einsum('bhqk,bkhd->bqhd', a.astype(v.dtype), v)

    o = _sweep_query_blocks(block, q)
    return o.reshape(b, s, width) @ w_o


def _moe(h, router_w, router_b, w_gate_up, b_gate_up, w_down, b_down):
    b, s, d = h.shape
    t = h.reshape(b * s, d)
    n_tok = b * s
    n_assign = n_tok * TOP_K
    logits = (t @ router_w + router_b).astype(jnp.float32)
    top_logit, top_idx = lax.top_k(logits, TOP_K)
    gates = jax.nn.softmax(top_logit, axis=-1)
    flat_e = top_idx.reshape(-1)
    order = jnp.argsort(flat_e)
    sorted_e = flat_e[order]
    sorted_tok = order // TOP_K
    counts = jnp.bincount(flat_e, length=N_EXPERTS)
    padded = (counts + MOE_BLOCK - 1) // MOE_BLOCK * MOE_BLOCK
    pad_end = jnp.cumsum(padded)
    pad_start = pad_end - padded
    start = jnp.cumsum(counts) - counts
    dest = pad_start[sorted_e] + jnp.arange(n_assign, dtype=jnp.int32) - start[sorted_e]
    n_blocks = -(-n_assign // MOE_BLOCK) + N_EXPERTS
    n_slots = n_blocks * MOE_BLOCK
    slot_tok = jnp.full((n_slots,), n_tok, dtype=jnp.int32).at[dest].set(sorted_tok.astype(jnp.int32))
    block_start = jnp.arange(n_blocks, dtype=jnp.int32) * MOE_BLOCK
    block_expert = jnp.minimum(jnp.searchsorted(pad_end, block_start, side='right'), N_EXPERTS - 1)
    t_pad = jnp.concatenate([t, jnp.zeros((1, d), t.dtype)], axis=0)
    xs = t_pad[slot_tok].reshape(n_blocks, MOE_BLOCK, d)

    def expert_block(args):
        xb, e = args
        gu = xb @ w_gate_up[e] + b_gate_up[e]
        g = jnp.minimum(gu[:, :D_EXPERT], SWIGLU_LIMIT)
        lin = jnp.clip(gu[:, D_EXPERT:], -SWIGLU_LIMIT, SWIGLU_LIMIT)
        act = g * jax.nn.sigmoid(SWIGLU_ALPHA * g) * (lin + 1.0)
        return act @ w_down[e] + b_down[e]

    ys = lax.map(expert_block, (xs, block_expert)).reshape(n_slots, d)
    y_assign = ys[dest] * gates.reshape(-1)[order].astype(h.dtype)[:, None]
    out = jax.ops.segment_sum(y_assign, sorted_tok, num_segments=n_tok)
    return out.reshape(b, s, d)


def setup_inputs(seed: int = 0) -> dict:
    key = jax.random.key(seed)
    ks = jax.random.split(key, 32)

    def nrm(k, shape, std):
        return jax.random.normal(k, shape, jnp.float32) * std

    d = D_MODEL
    mla_in = MLA_Q_RANK + MLA_KV_RANK + MLA_ROPE_DIM
    gate_offset = jnp.repeat(jnp.array([0.0, 0.0, 1.0, 0.0, 0.0, 1.0], jnp.float32), d)
    return {
        "x": nrm(ks[0], (BATCH, SEQ, d), 1.0),
        "c": nrm(ks[1], (BATCH, d), 1.0),
        "ada_w": nrm(ks[2], (DEPTH, d, 6 * d), 0.1 * d ** -0.5),
        "ada_b": nrm(ks[3], (DEPTH, 6 * d), 0.02) + gate_offset,
        "ln_g": 1.0 + nrm(ks[4], (DEPTH, 2, d), 0.02),
        "ln_b": nrm(ks[5], (DEPTH, 2, d), 0.02),
        "mla_w_in": nrm(ks[6], (N_MLA_LAYERS, d, mla_in), d ** -0.5),
        "mla_q_norm_g": 1.0 + nrm(ks[7], (N_MLA_LAYERS, MLA_Q_RANK), 0.02),
        "mla_kv_norm_g": 1.0 + nrm(ks[8], (N_MLA_LAYERS, MLA_KV_RANK), 0.02),
        "mla_w_uq": nrm(ks[9], (N_MLA_LAYERS, MLA_Q_RANK, MLA_HEADS * (MLA_NOPE_DIM + MLA_ROPE_DIM)), MLA_Q_RANK ** -0.5),
        "mla_w_ukv": nrm(ks[10], (N_MLA_LAYERS, MLA_KV_RANK, MLA_HEADS * (MLA_NOPE_DIM + MLA_V_DIM)), MLA_KV_RANK ** -0.5),
        "mla_w_o": nrm(ks[11], (N_MLA_LAYERS, MLA_HEADS * MLA_V_DIM, d), DEEPNORM_BETA * (MLA_HEADS * MLA_V_DIM) ** -0.5),
        "diff_w_qkv": nrm(ks[12], (N_DIFF_LAYERS, d, 3 * DIFF_HEADS * 2 * DIFF_HEAD_DIM), d ** -0.5),
        "diff_lambda": nrm(ks[13], (N_DIFF_LAYERS, 4, DIFF_HEAD_DIM), 0.1),
        "diff_subln_g": 1.0 + nrm(ks[14], (N_DIFF_LAYERS, 2 * DIFF_HEAD_DIM), 0.02),
        "diff_w_o": nrm(ks[15], (N_DIFF_LAYERS, DIFF_HEADS * 2 * DIFF_HEAD_DIM, d), DEEPNORM_BETA * (DIFF_HEADS * 2 * DIFF_HEAD_DIM) ** -0.5),
        "sb_w_qkv": nrm(ks[16], (N_SB_LAYERS, d, 3 * SB_HEADS * SB_HEAD_DIM), d ** -0.5),
        "sb_w_o": nrm(ks[17], (N_SB_LAYERS, SB_HEADS * SB_HEAD_DIM, d), DEEPNORM_BETA * (SB_HEADS * SB_HEAD_DIM) ** -0.5),
        "moe_router_w": nrm(ks[18], (DEPTH, d, N_EXPERTS), d ** -0.5),
        "moe_router_b": nrm(ks[19], (DEPTH, N_EXPERTS), 0.01),
        "moe_w_gate_up": nrm(ks[20], (DEPTH, N_EXPERTS, d, 2 * D_EXPERT), d ** -0.5),
        "moe_b_gate_up": nrm(ks[21], (DEPTH, N_EXPERTS, 2 * D_EXPERT), 0.01),
        "moe_w_down": nrm(ks[22], (DEPTH, N_EXPERTS, D_EXPERT, d), DEEPNORM_BETA * D_EXPERT ** -0.5),
        "moe_b_down": nrm(ks[23], (DEPTH, N_EXPERTS, d), 0.01),
    }


def reference(x, c, ada_w, ada_b, ln_g, ln_b, mla_w_in, mla_q_norm_g, mla_kv_norm_g, mla_w_uq,
              mla_w_ukv, mla_w_o, diff_w_qkv, diff_lambda, diff_subln_g, diff_w_o, sb_w_qkv, sb_w_o,
              moe_router_w, moe_router_b, moe_w_gate_up, moe_b_gate_up, moe_w_down, moe_b_down):
    pos = jnp.arange(x.shape[1], dtype=jnp.int32)
    cond = jax.nn.silu(c)
    for i in range(DEPTH):
        mod = (cond @ ada_w[i] + ada_b[i])[:, None, :]
        shift_t, scale_t, gate_t, shift_c, scale_c, gate_c = jnp.split(mod, 6, axis=-1)
        h = x * (1.0 + scale_t) + shift_t
        kind, j = i % N_MIXERS, i // N_MIXERS
        if kind == 0:
            y = _mla(h, pos, mla_w_in[j], mla_q_norm_g[j], mla_kv_norm_g[j], mla_w_uq[j], mla_w_ukv[j], mla_w_o[j])
        elif kind == 1:
            y = _diff_attention(h, pos, i, diff_w_qkv[j], diff_lambda[j], diff_subln_g[j], diff_w_o[j])
        else:
            y = _stick_breaking(h, sb_w_qkv[j], sb_w_o[j])
        x = _layer_norm(DEEPNORM_ALPHA * x + gate_t * y, ln_g[i, 0], ln_b[i, 0])
        h = x * (1.0 + scale_c) + shift_c
        y = _moe(h, moe_router_w[i], moe_router_b[i], moe_w_gate_up[i], moe_b_gate_up[i], moe_w_down[i], moe_b_down[i])
        x = _layer_norm(DEEPNORM_ALPHA * x + gate_c * y, ln_g[i, 1], ln_b[i, 1])
    return x
```

```python
import functools
import math

import jax
import jax.numpy as jnp
from jax import lax
from jax.experimental import pallas as pl
from jax.experimental.pallas import tpu as pltpu

F32 = jnp.float32
BF16 = jnp.bfloat16

N_MIXERS = 3
NEG_INF = -1e30
ROPE_THETA = 10000.0
LN_EPS = 1e-5
RMS_EPS = 1e-6
HEAD_DIM = 128
MLA_ROPE_DIM = 64
TOP_K = 4
SWIGLU_LIMIT = 7.0
SWIGLU_ALPHA = 1.702
LANES = 128
MIB = 1024 * 1024


def _cparams(sems, vmem_mib):
    return pltpu.CompilerParams(dimension_semantics=sems, vmem_limit_bytes=vmem_mib * MIB)


def _layer_norm_rows(r, g, b):
    mu = jnp.mean(r, axis=-1, keepdims=True)
    d = r - mu
    var = jnp.mean(d * d, axis=-1, keepdims=True)
    return d * lax.rsqrt(var + LN_EPS) * g + b


def _adaln_kernel(c_ref, w_ref, b_ref, o_ref):
    c = c_ref[...]
    cond = c * jax.nn.sigmoid(c)
    o_ref[...] = jnp.dot(cond.astype(BF16), w_ref[...].astype(BF16),
                         preferred_element_type=F32) + b_ref[...]


def _adaln(c, ada_w, ada_b):
    depth, d, n = ada_w.shape
    b = c.shape[0]
    rows = 16
    tn = next(cand for cand in (1024, 512, 256, 128) if n % cand == 0)
    c_pad = jnp.zeros((rows, d), F32).at[:b].set(c)
    out = pl.pallas_call(
        _adaln_kernel,
        grid=(depth, n // tn),
        in_specs=[
            pl.BlockSpec((rows, d), lambda l, j: (0, 0)),
            pl.BlockSpec((None, d, tn), lambda l, j: (l, 0, j)),
            pl.BlockSpec((None, 1, tn), lambda l, j: (l, 0, j)),
        ],
        out_specs=pl.BlockSpec((None, rows, tn), lambda l, j: (l, 0, j)),
        out_shape=jax.ShapeDtypeStruct((depth, rows, n), F32),
        compiler_params=_cparams(("parallel", "parallel"), 40),
        name="adaln",
    )(c_pad, ada_w, ada_b.reshape(depth, 1, n))
    return out[:, :b].reshape(depth, b, 6, d)


def _proj_kernel(*refs, has_mod, n_extra, n_out, epilogue):
    x_ref = refs[0]
    pos = 1
    mod_ref = None
    if has_mod:
        mod_ref = refs[pos]
        pos += 1
    w_ref = refs[pos]
    pos += 1
    extra = refs[pos:pos + n_extra]
    pos += n_extra
    outs = refs[pos:pos + n_out]
    pos += n_out
    if has_mod:
        h_ref = refs[pos]

        @pl.when(pl.program_id(1) == 0)
        def _():
            mod = mod_ref[...]
            h_ref[...] = (x_ref[...] * (1.0 + mod[1:2, :]) + mod[0:1, :]).astype(BF16)

        h = h_ref[...]
    else:
        h = x_ref[...]
    acc = jnp.dot(h, w_ref[...], preferred_element_type=F32)
    epilogue(acc, pl.program_id(1), extra, outs)


def _proj(x, w, *, tm, tn, mod, seq, extras, outs, epilogue, vmem_mib=48, name):
    t, k = x.shape
    n = w.shape[1]
    in_specs = [pl.BlockSpec((tm, k), lambda i, j: (i, 0))]
    args = [x]
    if mod is not None:
        blocks_per_seq = seq // tm
        in_specs.append(pl.BlockSpec((None, 6, k), lambda i, j: (i // blocks_per_seq, 0, 0)))
        args.append(mod)
    in_specs.append(pl.BlockSpec((k, tn), lambda i, j: (0, j)))
    args.append(w)
    for arr, blk, imap in extras:
        in_specs.append(pl.BlockSpec(blk, imap))
        args.append(arr)
    out_specs = [pl.BlockSpec(blk, imap) for _, _, blk, imap in outs]
    out_shape = [jax.ShapeDtypeStruct(shp, dt) for shp, dt, _, _ in outs]
    scratch = [pltpu.VMEM((tm, k), BF16)] if mod is not None else []
    kern = functools.partial(_proj_kernel, has_mod=mod is not None, n_extra=len(extras),
                             n_out=len(outs), epilogue=epilogue)
    return pl.pallas_call(
        kern,
        grid=(t // tm, n // tn),
        in_specs=in_specs,
        out_specs=out_specs,
        out_shape=out_shape,
        scratch_shapes=scratch,
        compiler_params=_cparams(("parallel", "arbitrary"), vmem_mib),
        name=name,
    )(*args)


def _rope_tables(seq, dim):
    half = dim // 2
    inv_freq = ROPE_THETA ** (-jnp.arange(half, dtype=F32) * (2.0 / dim))
    ang = jnp.arange(seq, dtype=F32)[:, None] * inv_freq[None, :]
    return jnp.cos(ang), jnp.sin(ang)


def _mla_rope_tables(seq, scale):
    cos, sin = _rope_tables(seq, MLA_ROPE_DIM)
    z32 = jnp.zeros_like(cos)
    z64 = jnp.zeros((seq, LANES - MLA_ROPE_DIM), F32)
    c = jnp.concatenate([cos, cos, z64], axis=1) * scale
    s_left = jnp.concatenate([-sin, z32, z64], axis=1) * scale
    s_right = jnp.concatenate([z32, sin, z64], axis=1) * scale
    return c, s_left, s_right


def _rope64_in_chunk(r, c, s_left, s_right):
    half = MLA_ROPE_DIM // 2
    return (r * c + pltpu.roll(r, LANES - half, 1) * s_left + pltpu.roll(r, half, 1) * s_right)


def _mla_lat_epilogue(acc, j, extra, outs, *, q_rank, kv_rank):
    gq_ref, gkv_ref, c_ref, sl_ref, sr_ref = extra
    cq_ref, ckv_ref, kr_ref = outs

    def rms(v, g):
        return v * lax.rsqrt(jnp.mean(v * v, axis=-1, keepdims=True) + RMS_EPS) * g

    cq_ref[...] = rms(acc[:, :q_rank], gq_ref[...]).astype(BF16)
    ckv_ref[...] = rms(acc[:, q_rank:q_rank + kv_rank], gkv_ref[...]).astype(BF16)
    kr = acc[:, q_rank + kv_rank:]
    kr_ref[...] = _rope64_in_chunk(kr, c_ref[...], sl_ref[...], sr_ref[...]).astype(BF16)


def _mla_q_epilogue(acc, j, extra, outs, *, heads_per_tile, scale):
    c_ref, sl_ref, sr_ref = extra
    (q_ref,) = outs
    c, sl, sr = c_ref[...], sl_ref[...], sr_ref[...]
    for h in range(heads_per_tile):
        lo = h * 2 * LANES
        q_ref[:, lo:lo + LANES] = (acc[:, lo:lo + LANES] * scale).astype(BF16)
        q_ref[:, lo + LANES:lo + 2 * LANES] = _rope64_in_chunk(
            acc[:, lo + LANES:lo + 2 * LANES], c, sl, sr).astype(BF16)


def _mla_kv_kernel(ckv_ref, wk_ref, wv_ref, kr_ref, kcat_ref, v_ref, *, heads_per_tile):
    x = ckv_ref[...]
    kn = jnp.dot(x, wk_ref[...], preferred_element_type=F32)
    kr = kr_ref[...]
    for h in range(heads_per_tile):
        kcat_ref[:, h * 2 * LANES:h * 2 * LANES + LANES] = kn[:, h * LANES:(h + 1) * LANES].astype(BF16)
        kcat_ref[:, h * 2 * LANES + LANES:(h + 1) * 2 * LANES] = kr
    v_ref[...] = jnp.dot(x, wv_ref[...], preferred_element_type=F32).astype(BF16)


def _causal_mask(tq, tk):
    row = lax.broadcasted_iota(jnp.int32, (tq, tk), 0)
    col = lax.broadcasted_iota(jnp.int32, (tq, tk), 1)
    return col <= row


def _qk(q, k):
    return lax.dot_general(q, k, (((1,), (1,)), ((), ())), preferred_element_type=F32)


def _softmax_step(s, v, m_ref, l_ref, acc_ref):
    m_prev = m_ref[...]
    m_new = jnp.maximum(m_prev, jnp.max(s, axis=-1, keepdims=True))
    alpha = jnp.exp(m_prev - m_new)
    p = jnp.exp(s - m_new)
    l_ref[...] = alpha * l_ref[...] + jnp.sum(p, axis=-1, keepdims=True)
    acc_ref[...] = alpha * acc_ref[...] + jnp.dot(p.astype(BF16), v, preferred_element_type=F32)
    m_ref[...] = m_new


def _mla_attn_kernel(q_ref, k_ref, v_ref, o_ref, m_ref, l_ref, acc_ref, *, blk, seq):
    def q_body(qi, carry):
        q0 = pl.multiple_of(qi * blk, blk)
        q = q_ref[pl.ds(q0, blk), :]
        m_ref[...] = jnp.full(m_ref.shape, NEG_INF, F32)
        l_ref[...] = jnp.zeros(l_ref.shape, F32)
        acc_ref[...] = jnp.zeros(acc_ref.shape, F32)

        def kv_body(kj, c):
            k0 = pl.multiple_of(kj * blk, blk)
            s = _qk(q, k_ref[pl.ds(k0, blk), :])
            _softmax_step(s, v_ref[pl.ds(k0, blk), :], m_ref, l_ref, acc_ref)
            return c

        lax.fori_loop(0, qi, kv_body, 0)
        s = jnp.where(_causal_mask(blk, blk), _qk(q, k_ref[pl.ds(q0, blk), :]), NEG_INF)
        _softmax_step(s, v_ref[pl.ds(q0, blk), :], m_ref, l_ref, acc_ref)
        o_ref[pl.ds(q0, blk), :] = (acc_ref[...] / l_ref[...]).astype(o_ref.dtype)
        return carry

    lax.fori_loop(0, seq // blk, q_body, 0)


def _diff_attn_kernel(q_ref, k_ref, v_ref, lam_ref, g_ref, o_ref,
                      m1_ref, l1_ref, a1_ref, m2_ref, l2_ref, a2_ref, *, blk, seq, lam_init):
    hd = HEAD_DIM
    lf = lam_ref[...]
    lam_full = (jnp.exp(jnp.sum(lf[0:1, :] * lf[1:2, :], axis=-1, keepdims=True))
                - jnp.exp(jnp.sum(lf[2:3, :] * lf[3:4, :], axis=-1, keepdims=True)) + lam_init)

    def q_body(qi, carry):
        q0 = pl.multiple_of(qi * blk, blk)
        q = q_ref[pl.ds(q0, blk), :]
        q1, q2 = q[:, :hd], q[:, hd:]
        for m_ref, l_ref, a_ref in ((m1_ref, l1_ref, a1_ref), (m2_ref, l2_ref, a2_ref)):
            m_ref[...] = jnp.full(m_ref.shape, NEG_INF, F32)
            l_ref[...] = jnp.zeros(l_ref.shape, F32)
            a_ref[...] = jnp.zeros(a_ref.shape, F32)

        def step(k0, masked):
            k = k_ref[pl.ds(k0, blk), :]
            v = v_ref[pl.ds(k0, blk), :]
            s1 = _qk(q1, k[:, :hd])
            s2 = _qk(q2, k[:, hd:])
            if masked:
                mask = _causal_mask(blk, blk)
                s1 = jnp.where(mask, s1, NEG_INF)
                s2 = jnp.where(mask, s2, NEG_INF)
            _softmax_step(s1, v, m1_ref, l1_ref, a1_ref)
            _softmax_step(s2, v, m2_ref, l2_ref, a2_ref)

        def kv_body(kj, c):
            step(pl.multiple_of(kj * blk, blk), False)
            return c

        lax.fori_loop(0, qi, kv_body, 0)
        step(q0, True)
        o = a1_ref[...] / l1_ref[...] - lam_full * (a2_ref[...] / l2_ref[...])
        o = o * lax.rsqrt(jnp.mean(o * o, axis=-1, keepdims=True) + LN_EPS) * g_ref[...]
        o_ref[pl.ds(q0, blk), :] = (o * (1.0 - lam_init)).astype(o_ref.dtype)
        return carry

    lax.fori_loop(0, seq // blk, q_body, 0)


def _log_sigmoid(z):
    return jnp.minimum(z, 0.0) - jnp.log1p(jnp.exp(-jnp.abs(z)))


def _sb_attn_kernel(q_ref, k_ref, v_ref, o_ref, run_ref, acc_ref, *, tq, tk, seq):
    chunks_per_q = tq // tk
    jr = lax.broadcasted_iota(jnp.int32, (tk, tk), 0)
    jc = lax.broadcasted_iota(jnp.int32, (tk, tk), 1)
    later_mat = (jr > jc).astype(BF16)

    def q_body(qi, carry):
        q0 = pl.multiple_of(qi * tq, tq)
        q = q_ref[pl.ds(q0, tq), :]
        run_ref[...] = jnp.zeros(run_ref.shape, F32)
        acc_ref[...] = jnp.zeros(acc_ref.shape, F32)

        def step(k0, diag_offset):
            z = _qk(q, k_ref[pl.ds(k0, tk), :])
            log_beta = _log_sigmoid(z)
            log_keep = log_beta - z
            if diag_offset is not None:
                row = lax.broadcasted_iota(jnp.int32, (tq, tk), 0)
                col = lax.broadcasted_iota(jnp.int32, (tq, tk), 1) + diag_offset
                strict = col < row
                log_keep = jnp.where(strict, log_keep, 0.0)
            hi = log_keep.astype(BF16)
            lo = (log_keep - hi.astype(F32)).astype(BF16)
            later = (jnp.dot(hi, later_mat, preferred_element_type=F32)
                     + jnp.dot(lo, later_mat, preferred_element_type=F32) + run_ref[...])
            a = jnp.exp(log_beta + later)
            if diag_offset is not None:
                a = jnp.where(strict, a, 0.0)
            acc_ref[...] += jnp.dot(a.astype(BF16), v_ref[pl.ds(k0, tk), :],
                                    preferred_element_type=F32)
            run_ref[...] += jnp.sum(log_keep, axis=-1, keepdims=True)

        for c in reversed(range(chunks_per_q)):
            step(q0 + c * tk, c * tk)

        def kv_body(n, c):
            kj = qi * chunks_per_q - 1 - n
            step(pl.multiple_of(kj * tk, tk), None)
            return c

        lax.fori_loop(0, qi * chunks_per_q, kv_body, 0)
        o_ref[pl.ds(q0, tq), :] = acc_ref[...].astype(o_ref.dtype)
        return carry

    lax.fori_loop(0, seq // tq, q_body, 0)


def _attention(kernel, q, k, v, extras, *, batch, seq, heads, dq, dv, scratch, vmem_mib, name):
    in_specs = [
        pl.BlockSpec((seq, dq), lambda b, h: (b, h)),
        pl.BlockSpec((seq, dq), lambda b, h: (b, h)),
        pl.BlockSpec((seq, dv), lambda b, h: (b, h)),
    ]
    for arr in extras:
        in_specs.append(pl.BlockSpec(arr.shape, lambda b, h: (0, 0)))
    return pl.pallas_call(
        kernel,
        grid=(batch, heads),
        in_specs=in_specs,
        out_specs=pl.BlockSpec((seq, dv), lambda b, h: (b, h)),
        out_shape=jax.ShapeDtypeStruct((batch * seq, heads * dv), BF16),
        scratch_shapes=scratch,
        compiler_params=_cparams(("parallel", "parallel"), vmem_mib),
        name=name,
    )(q, k, v, *extras)


def _mla_mixer(x2d, mod, seq, batch, w_in, q_norm_g, kv_norm_g, w_uq, w_ukv):
    t, d = x2d.shape
    q_rank, kv_rank = q_norm_g.shape[0], kv_norm_g.shape[0]
    heads = w_uq.shape[1] // (HEAD_DIM + MLA_ROPE_DIM)
    scale = (HEAD_DIM + MLA_ROPE_DIM) ** -0.5
    tm = min(512, seq)
    row_blocks = seq // tm

    lat_n = q_rank + kv_rank + LANES
    w_in_p = jnp.zeros((d, lat_n), BF16).at[:, :w_in.shape[1]].set(w_in.astype(BF16))
    c_k, sl_k, sr_k = _mla_rope_tables(seq, 1.0)
    c_q, sl_q, sr_q = _mla_rope_tables(seq, scale)
    tab_spec = ((tm, LANES), lambda i, j: (i % row_blocks, 0))

    cq, ckv, kr = _proj(
        x2d, w_in_p, tm=tm, tn=lat_n, mod=mod, seq=seq,
        extras=[(q_norm_g.reshape(1, q_rank), (1, q_rank), lambda i, j: (0, 0)),
                (kv_norm_g.reshape(1, kv_rank), (1, kv_rank), lambda i, j: (0, 0)),
                (c_k,) + tab_spec, (sl_k,) + tab_spec, (sr_k,) + tab_spec],
        outs=[((t, q_rank), BF16, (tm, q_rank), lambda i, j: (i, 0)),
              ((t, kv_rank), BF16, (tm, kv_rank), lambda i, j: (i, 0)),
              ((t, LANES), BF16, (tm, LANES), lambda i, j: (i, 0))],
        epilogue=functools.partial(_mla_lat_epilogue, q_rank=q_rank, kv_rank=kv_rank),
        name="mla_latent")

    w_q = w_uq.reshape(q_rank, heads, HEAD_DIM + MLA_ROPE_DIM).astype(BF16)
    w_q = jnp.pad(w_q, ((0, 0), (0, 0), (0, 2 * LANES - HEAD_DIM - MLA_ROPE_DIM)))
    w_q = w_q.reshape(q_rank, heads * 2 * LANES)
    hpt = min(heads, 4)
    (q_cat,) = _proj(
        cq, w_q, tm=tm, tn=hpt * 2 * LANES, mod=None, seq=seq,
        extras=[(c_q,) + tab_spec, (sl_q,) + tab_spec, (sr_q,) + tab_spec],
        outs=[((t, heads * 2 * LANES), BF16, (tm, hpt * 2 * LANES), lambda i, j: (i, j))],
        epilogue=functools.partial(_mla_q_epilogue, heads_per_tile=hpt, scale=scale),
        name="mla_q_up")

    w_kv = w_ukv.reshape(kv_rank, heads, 2 * HEAD_DIM).astype(BF16)
    w_k = w_kv[:, :, :HEAD_DIM].reshape(kv_rank, heads * HEAD_DIM)
    w_v = w_kv[:, :, HEAD_DIM:].reshape(kv_rank, heads * HEAD_DIM)
    k_cat, v = pl.pallas_call(
        functools.partial(_mla_kv_kernel, heads_per_tile=hpt),
        grid=(t // tm, heads // hpt),
        in_specs=[
            pl.BlockSpec((tm, kv_rank), lambda i, j: (i, 0)),
            pl.BlockSpec((kv_rank, hpt * HEAD_DIM), lambda i, j: (0, j)),
            pl.BlockSpec((kv_rank, hpt * HEAD_DIM), lambda i, j: (0, j)),
            pl.BlockSpec((tm, LANES), lambda i, j: (i, 0)),
        ],
        out_specs=[
            pl.BlockSpec((tm, hpt * 2 * LANES), lambda i, j: (i, j)),
            pl.BlockSpec((tm, hpt * HEAD_DIM), lambda i, j: (i, j)),
        ],
        out_shape=[jax.ShapeDtypeStruct((t, heads * 2 * LANES), BF16),
                   jax.ShapeDtypeStruct((t, heads * HEAD_DIM), BF16)],
        compiler_params=_cparams(("parallel", "parallel"), 32),
        name="mla_kv_up",
    )(ckv, w_k, w_v, kr)

    blk = min(512, seq)
    return _attention(
        functools.partial(_mla_attn_kernel, blk=blk, seq=seq), q_cat, k_cat, v, [],
        batch=batch, seq=seq, heads=heads, dq=2 * LANES, dv=HEAD_DIM,
        scratch=[pltpu.VMEM((blk, 1), F32), pltpu.VMEM((blk, 1), F32), pltpu.VMEM((blk, HEAD_DIM), F32)],
        vmem_mib=40, name="mla_attention")


def _diff_qkv_epilogue(acc, j, extra, outs, *, tn, width, scale):
    cos_ref, sin_ref = extra
    (o_ref,) = outs

    def rope(factor):
        cos = cos_ref[...] * factor
        sin = sin_ref[...] * factor
        for c in range(tn // LANES):
            a = acc[:, c * LANES:(c + 1) * LANES]
            o_ref[:, c * LANES:(c + 1) * LANES] = (
                a * cos + pltpu.roll(a, HEAD_DIM // 2, 1) * sin).astype(BF16)

    @pl.when(j * tn < width)
    def _():
        rope(scale)

    @pl.when(jnp.logical_and(j * tn >= width, j * tn < 2 * width))
    def _():
        rope(1.0)

    @pl.when(j * tn >= 2 * width)
    def _():
        o_ref[...] = acc.astype(BF16)


def _diff_mixer(x2d, mod, seq, batch, layer_idx, w_qkv, lam, subln_g):
    t, d = x2d.shape
    width = w_qkv.shape[1] // 3
    heads = width // (2 * HEAD_DIM)
    scale = HEAD_DIM ** -0.5
    lam_init = 0.8 - 0.6 * math.exp(-0.3 * layer_idx)
    tm = min(512, seq)
    tn = min(1024, width)
    row_blocks = seq // tm
    cos, sin = _rope_tables(seq, HEAD_DIM)
    cos_t = jnp.concatenate([cos, cos], axis=1)
    sin_t = jnp.concatenate([-sin, sin], axis=1)
    tab_spec = ((tm, LANES), lambda i, j: (i % row_blocks, 0))
    (qkv,) = _proj(
        x2d, w_qkv.astype(BF16), tm=tm, tn=tn, mod=mod, seq=seq,
        extras=[(cos_t,) + tab_spec, (sin_t,) + tab_spec],
        outs=[((t, 3 * width), BF16, (tm, tn), lambda i, j: (i, j))],
        epilogue=functools.partial(_diff_qkv_epilogue, tn=tn, width=width, scale=scale),
        name="diff_qkv")
    blk = min(512, seq)
    dh = 2 * HEAD_DIM
    in_specs = [
        pl.BlockSpec((seq, dh), lambda b, h: (b, h)),
        pl.BlockSpec((seq, dh), lambda b, h: (b, heads + h)),
        pl.BlockSpec((seq, dh), lambda b, h: (b, 2 * heads + h)),
        pl.BlockSpec((4, HEAD_DIM), lambda b, h: (0, 0)),
        pl.BlockSpec((1, dh), lambda b, h: (0, 0)),
    ]
    stat = lambda: pltpu.VMEM((blk, 1), F32)
    accum = lambda: pltpu.VMEM((blk, dh), F32)
    return pl.pallas_call(
        functools.partial(_diff_attn_kernel, blk=blk, seq=seq, lam_init=lam_init),
        grid=(batch, heads),
        in_specs=in_specs,
        out_specs=pl.BlockSpec((seq, dh), lambda b, h: (b, h)),
        out_shape=jax.ShapeDtypeStruct((t, width), BF16),
        scratch_shapes=[stat(), stat(), accum(), stat(), stat(), accum()],
        compiler_params=_cparams(("parallel", "parallel"), 40),
        name="diff_attention",
    )(qkv, qkv, qkv, lam, subln_g.reshape(1, dh))


def _sb_qkv_epilogue(acc, j, extra, outs, *, tn, width, scale):
    (o_ref,) = outs

    @pl.when(j * tn < width)
    def _():
        o_ref[...] = (acc * scale).astype(BF16)

    @pl.when(j * tn >= width)
    def _():
        o_ref[...] = acc.astype(BF16)


def _sb_mixer(x2d, mod, seq, batch, w_qkv):
    t, d = x2d.shape
    width = w_qkv.shape[1] // 3
    heads = width // HEAD_DIM
    scale = HEAD_DIM ** -0.5
    tm = min(512, seq)
    tn = min(1024, width)
    (qkv,) = _proj(
        x2d, w_qkv.astype(BF16), tm=tm, tn=tn, mod=mod, seq=seq, extras=[],
        outs=[((t, 3 * width), BF16, (tm, tn), lambda i, j: (i, j))],
        epilogue=functools.partial(_sb_qkv_epilogue, tn=tn, width=width, scale=scale),
        name="sb_qkv")
    tq = min(512, seq)
    tk = min(256, seq)
    in_specs = [
        pl.BlockSpec((seq, HEAD_DIM), lambda b, h: (b, h)),
        pl.BlockSpec((seq, HEAD_DIM), lambda b, h: (b, heads + h)),
        pl.BlockSpec((seq, HEAD_DIM), lambda b, h: (b, 2 * heads + h)),
    ]
    return pl.pallas_call(
        functools.partial(_sb_attn_kernel, tq=tq, tk=tk, seq=seq),
        grid=(batch, heads),
        in_specs=in_specs,
        out_specs=pl.BlockSpec((seq, HEAD_DIM), lambda b, h: (b, h)),
        out_shape=jax.ShapeDtypeStruct((t, width), BF16),
        scratch_shapes=[pltpu.VMEM((tq, 1), F32), pltpu.VMEM((tq, HEAD_DIM), F32)],
        compiler_params=_cparams(("parallel", "parallel"), 40),
        name="sb_attention",
    )(qkv, qkv, qkv)


def _outproj_router_kernel(o_ref, w_ref, x_ref, mod_ref, g_ref, b_ref, rw_ref, rb_ref,
                           x1_ref, h2_ref, idx_ref, gate_ref, *, alpha, n_experts):
    y = jnp.dot(o_ref[...], w_ref[...], preferred_element_type=F32)
    mod = mod_ref[...]
    x1 = _layer_norm_rows(alpha * x_ref[...] + mod[2:3, :] * y, g_ref[...], b_ref[...])
    x1_ref[...] = x1
    h2 = x1 * (1.0 + mod[4:5, :]) + mod[3:4, :]
    h2_ref[...] = h2
    logits = jnp.dot(h2, rw_ref[...], preferred_element_type=F32,
                     precision=lax.Precision.HIGHEST) + rb_ref[...]
    lane = lax.broadcasted_iota(jnp.int32, logits.shape, 1).astype(F32)
    work = jnp.where(lane < n_experts, logits, -jnp.inf)
    idx_out = jnp.zeros(logits.shape, F32)
    val_out = jnp.zeros(logits.shape, F32)
    top = None
    denom = None
    for kk in range(TOP_K):
        m = jnp.max(work, axis=-1, keepdims=True)
        first = jnp.min(jnp.where(work == m, lane, float(LANES)), axis=-1, keepdims=True)
        if kk == 0:
            top = m
        e = jnp.exp(m - top)
        denom = e if kk == 0 else denom + e
        idx_out = jnp.where(lane == kk, first, idx_out)
        val_out = jnp.where(lane == kk, e, val_out)
        work = jnp.where(lane == first, -jnp.inf, work)
    idx_ref[...] = idx_out.astype(jnp.int32)
    gate_ref[...] = val_out / denom


def _outproj_router(o, w_o, x2d, mod, ln_g, ln_b, router_w, router_b, *, seq, alpha):
    t, d = x2d.shape
    n_experts = router_w.shape[1]
    tm = min(256, seq)
    blocks_per_seq = seq // tm
    rw = jnp.zeros((d, LANES), F32).at[:, :n_experts].set(router_w)
    rb = jnp.zeros((1, LANES), F32).at[0, :n_experts].set(router_b)
    row = lambda i: (i, 0)
    fixed = lambda i: (0, 0)
    return pl.pallas_call(
        functools.partial(_outproj_router_kernel, alpha=alpha, n_experts=n_experts),
        grid=(t // tm,),
        in_specs=[
            pl.BlockSpec((tm, o.shape[1]), row),
            pl.BlockSpec(w_o.shape, fixed),
            pl.BlockSpec((tm, d), row),
            pl.BlockSpec((None, 6, d), lambda i: (i // blocks_per_seq, 0, 0)),
            pl.BlockSpec((1, d), fixed),
            pl.BlockSpec((1, d), fixed),
            pl.BlockSpec((d, LANES), fixed),
            pl.BlockSpec((1, LANES), fixed),
        ],
        out_specs=[pl.BlockSpec((tm, d), row), pl.BlockSpec((tm, d), row),
                   pl.BlockSpec((tm, LANES), row), pl.BlockSpec((tm, LANES), row)],
        out_shape=[jax.ShapeDtypeStruct((t, d), F32), jax.ShapeDtypeStruct((t, d), F32),
                   jax.ShapeDtypeStruct((t, LANES), jnp.int32), jax.ShapeDtypeStruct((t, LANES), F32)],
        compiler_params=_cparams(("parallel",), 48),
        name="outproj_ln_router",
    )(o, w_o.astype(BF16), x2d, mod, ln_g.reshape(1, d), ln_b.reshape(1, d), rw, rb)


def _routing_plan(top_idx, gates, n_experts, tm):
    t = top_idx.shape[0]
    n_assign = t * TOP_K
    flat_e = top_idx.reshape(n_assign)
    onehot = (flat_e[:, None] == jnp.arange(n_experts, dtype=jnp.int32)[None, :]).astype(jnp.int32)
    csum = jnp.cumsum(onehot, axis=0)
    pos = jnp.sum(csum * onehot, axis=1) - 1
    counts = csum[-1]
    padded = (counts + tm - 1) // tm * tm
    pad_end = jnp.cumsum(padded)
    pad_start = pad_end - padded
    dest = (jnp.sum(pad_start[None, :] * onehot, axis=1) + pos).astype(jnp.int32)
    n_blocks = n_assign // tm + n_experts
    n_slots = n_blocks * tm
    slot_tok = jnp.zeros((n_slots,), jnp.int32).at[dest].set(
        jnp.arange(n_assign, dtype=jnp.int32) // TOP_K)
    slot_gate = jnp.zeros((n_slots,), F32).at[dest].set(gates.reshape(n_assign))
    block_start = jnp.arange(n_blocks, dtype=jnp.int32) * tm
    block_expert = jnp.minimum(jnp.searchsorted(pad_end, block_start, side='right'),
                               n_experts - 1).astype(jnp.int32)
    n_used = (pad_end[-1] // tm).astype(jnp.int32).reshape(1)
    return dest, slot_tok, slot_gate, block_expert, n_used, n_blocks


def _expert_kernel(be_ref, tok_ref, used_ref, h_hbm, wgu_ref, bgu_ref, wd_ref, bd_ref, gate_ref,
                   y_ref, xbuf, sem, *, tm, d_expert, n_blocks):
    i = pl.program_id(0)
    slot = i % 2

    def issue(block, buf_slot):
        def body(r, c):
            tok = tok_ref[block * tm + r]
            pltpu.make_async_copy(h_hbm.at[pl.ds(tok, 1), :], xbuf.at[buf_slot, pl.ds(r, 1), :],
                                  sem.at[buf_slot]).start()
            return c
        lax.fori_loop(0, tm, body, 0, unroll=8)

    @pl.when(i == 0)
    def _():
        issue(0, 0)

    @pl.when(i + 1 < n_blocks)
    def _():
        issue(i + 1, 1 - slot)

    pltpu.make_async_copy(h_hbm.at[pl.ds(0, tm), :], xbuf.at[slot], sem.at[slot]).wait()

    @pl.when(i < used_ref[0])
    def _():
        x = xbuf[slot].astype(BF16)
        gu = jnp.dot(x, wgu_ref[...], preferred_element_type=F32) + bgu_ref[...]
        g = jnp.minimum(gu[:, :d_expert], SWIGLU_LIMIT)
        lin = jnp.clip(gu[:, d_expert:], -SWIGLU_LIMIT, SWIGLU_LIMIT)
        act = g * jax.nn.sigmoid(SWIGLU_ALPHA * g) * (lin + 1.0)
        y = jnp.dot(act.astype(BF16), wd_ref[...], preferred_element_type=F32) + bd_ref[...]
        y_ref[...] = y * gate_ref[...]

    @pl.when(i >= used_ref[0])
    def _():
        y_ref[...] = jnp.zeros(y_ref.shape, F32)


def _experts(h2, plan, w_gate_up, b_gate_up, w_down, b_down, tm):
    dest, slot_tok, slot_gate, block_expert, n_used, n_blocks = plan
    t, d = h2.shape
    n_experts, d_expert = w_down.shape[0], w_down.shape[1]
    grid_spec = pltpu.PrefetchScalarGridSpec(
        num_scalar_prefetch=3,
        grid=(n_blocks,),
        in_specs=[
            pl.BlockSpec(memory_space=pl.ANY),
            pl.BlockSpec((None, d, 2 * d_expert), lambda i, be, tok, used: (be[i], 0, 0)),
            pl.BlockSpec((None, 1, 2 * d_expert), lambda i, be, tok, used: (be[i], 0, 0)),
            pl.BlockSpec((None, d_expert, d), lambda i, be, tok, used: (be[i], 0, 0)),
            pl.BlockSpec((None, 1, d), lambda i, be, tok, used: (be[i], 0, 0)),
            pl.BlockSpec((tm, 1), lambda i, be, tok, used: (i, 0)),
        ],
        out_specs=pl.BlockSpec((tm, d), lambda i, be, tok, used: (i, 0)),
        scratch_shapes=[pltpu.VMEM((2, tm, d), F32), pltpu.SemaphoreType.DMA((2,))],
    )
    return pl.pallas_call(
        functools.partial(_expert_kernel, tm=tm, d_expert=d_expert, n_blocks=n_blocks),
        grid_spec=grid_spec,
        out_shape=jax.ShapeDtypeStruct((n_blocks * tm, d), F32),
        compiler_params=_cparams(("arbitrary",), 48),
        name="moe_experts",
    )(block_expert, slot_tok, n_used, h2, w_gate_up.astype(BF16),
      b_gate_up.reshape(n_experts, 1, 2 * d_expert), w_down.astype(BF16),
      b_down.reshape(n_experts, 1, d), slot_gate.reshape(-1, 1))


def _combine_kernel(dest_ref, ys_hbm, x_ref, mod_ref, g_ref, b_ref, o_ref, ybuf, sem,
                    *, tn, n_blocks, alpha):
    i = pl.program_id(0)
    slot = i % 2

    def issue(block, buf_slot):
        def body(r, c):
            base = (block * tn + r) * TOP_K
            for kk in range(TOP_K):
                src = dest_ref[base + kk]
                pltpu.make_async_copy(ys_hbm.at[pl.ds(src, 1), :],
                                      ybuf.at[buf_slot, kk, pl.ds(r, 1), :],
                                      sem.at[buf_slot]).start()
            return c
        lax.fori_loop(0, tn, body, 0, unroll=2)

    @pl.when(i == 0)
    def _():
        issue(0, 0)

    @pl.when(i + 1 < n_blocks)
    def _():
        issue(i + 1, 1 - slot)

    for kk in range(TOP_K):
        pltpu.make_async_copy(ys_hbm.at[pl.ds(0, tn), :], ybuf.at[slot, kk], sem.at[slot]).wait()

    y = ybuf[slot, 0] + ybuf[slot, 1] + ybuf[slot, 2] + ybuf[slot, 3]
    mod = mod_ref[...]
    o_ref[...] = _layer_norm_rows(alpha * x_ref[...] + mod[5:6, :] * y, g_ref[...], b_ref[...])


def _combine(ys, dest, x1, mod, ln_g, ln_b, *, seq, alpha):
    t, d = x1.shape
    tn = min(128, seq)
    n_blocks = t // tn
    blocks_per_seq = seq // tn
    grid_spec = pltpu.PrefetchScalarGridSpec(
        num_scalar_prefetch=1,
        grid=(n_blocks,),
        in_specs=[
            pl.BlockSpec(memory_space=pl.ANY),
            pl.BlockSpec((tn, d), lambda i, dst: (i, 0)),
            pl.BlockSpec((None, 6, d), lambda i, dst: (i // blocks_per_seq, 0, 0)),
            pl.BlockSpec((1, d), lambda i, dst: (0, 0)),
            pl.BlockSpec((1, d), lambda i, dst: (0, 0)),
        ],
        out_specs=pl.BlockSpec((tn, d), lambda i, dst: (i, 0)),
        scratch_shapes=[pltpu.VMEM((2, TOP_K, tn, d), F32), pltpu.SemaphoreType.DMA((2,))],
    )
    return pl.pallas_call(
        functools.partial(_combine_kernel, tn=tn, n_blocks=n_blocks, alpha=alpha),
        grid_spec=grid_spec,
        out_shape=jax.ShapeDtypeStruct((t, d), F32),
        compiler_params=_cparams(("arbitrary",), 40),
        name="moe_combine_ln",
    )(dest, ys, x1, mod, ln_g.reshape(1, d), ln_b.reshape(1, d))


def kernel(x, c, ada_w, ada_b, ln_g, ln_b, mla_w_in, mla_q_norm_g, mla_kv_norm_g, mla_w_uq, mla_w_ukv,
           mla_w_o, diff_w_qkv, diff_lambda, diff_subln_g, diff_w_o, sb_w_qkv, sb_w_o, moe_router_w,
           moe_router_b, moe_w_gate_up, moe_b_gate_up, moe_w_down, moe_b_down):
    batch, seq, d = x.shape
    depth = ada_w.shape[0]
    alpha = (2 * depth) ** 0.25
    n_experts = moe_router_w.shape[-1]
    moe_tm = 256
    mods = _adaln(c, ada_w, ada_b)
    x2d = x.reshape(batch * seq, d)
    for i in range(depth):
        mod = mods[i]
        kind, j = i % N_MIXERS, i // N_MIXERS
        if kind == 0:
            o = _mla_mixer(x2d, mod, seq, batch, mla_w_in[j], mla_q_norm_g[j], mla_kv_norm_g[j],
                           mla_w_uq[j], mla_w_ukv[j])
            w_o = mla_w_o[j]
        elif kind == 1:
            o = _diff_mixer(x2d, mod, seq, batch, i, diff_w_qkv[j], diff_lambda[j], diff_subln_g[j])
            w_o = diff_w_o[j]
        else:
            o = _sb_mixer(x2d, mod, seq, batch, sb_w_qkv[j])
            w_o = sb_w_o[j]
        x1, h2, idx_pad, gate_pad = _outproj_router(
            o, w_o, x2d, mod, ln_g[i, 0], ln_b[i, 0], moe_router_w[i], moe_router_b[i],
            seq=seq, alpha=alpha)
        plan = _routing_plan(idx_pad[:, :TOP_K], gate_pad[:, :TOP_K], n_experts, moe_tm)
        ys = _experts(h2, plan, moe_w_gate_up[i], moe_b_gate_up[i], moe_w_down[i], moe_b_down[i], moe_tm)
        x2d = _combine(ys, plan[0], x1, mod, ln_g[i, 1], ln_b[i, 1], seq=seq, alpha=alpha)
    return x2d.reshape(batch, seq, d)
```

```python
import functools
import math

import jax
import jax.numpy as jnp
from jax import lax
from jax.experimental import pallas as pl
from jax.experimental.pallas import tpu as pltpu

F32 = jnp.float32
BF16 = jnp.bfloat16

N_MIXERS = 3
NEG_INF = -1e30
ROPE_THETA = 10000.0
LN_EPS = 1e-5
RMS_EPS = 1e-6
HEAD_DIM = 128
MLA_ROPE_DIM = 64
TOP_K = 4
SWIGLU_LIMIT = 7.0
SWIGLU_ALPHA = 1.702
LANES = 128
SUBLANES = 8
LOG2_E = 1.4426950408889634
ATTN_HEAD_GROUP = 2
SB_DEAD_LOG2 = -160.0
MIB = 1024 * 1024


def _cparams(sems, vmem_mib):
    return pltpu.CompilerParams(dimension_semantics=sems, vmem_limit_bytes=vmem_mib * MIB)


def _layer_norm_rows(r, g, b):
    mu = jnp.mean(r, axis=-1, keepdims=True)
    d = r - mu
    var = jnp.mean(d * d, axis=-1, keepdims=True)
    return d * lax.rsqrt(var + LN_EPS) * g + b


def _adaln_kernel(c_ref, w_ref, b_ref, o_ref):
    c = c_ref[...]
    cond = c * jax.nn.sigmoid(c)
    o_ref[...] = jnp.dot(cond.astype(BF16), w_ref[...].astype(BF16),
                         preferred_element_type=F32) + b_ref[...]


def _adaln(c, ada_w, ada_b):
    depth, d, n = ada_w.shape
    b = c.shape[0]
    rows = 16
    tn = next(cand for cand in (1024, 512, 256, 128) if n % cand == 0)
    c_pad = jnp.zeros((rows, d), F32).at[:b].set(c)
    out = pl.pallas_call(
        _adaln_kernel,
        grid=(depth, n // tn),
        in_specs=[
            pl.BlockSpec((rows, d), lambda l, j: (0, 0)),
            pl.BlockSpec((None, d, tn), lambda l, j: (l, 0, j)),
            pl.BlockSpec((None, 1, tn), lambda l, j: (l, 0, j)),
        ],
        out_specs=pl.BlockSpec((None, rows, tn), lambda l, j: (l, 0, j)),
        out_shape=jax.ShapeDtypeStruct((depth, rows, n), F32),
        compiler_params=_cparams(("parallel", "parallel"), 40),
        name="adaln",
    )(c_pad, ada_w, ada_b.reshape(depth, 1, n))
    return out[:, :b].reshape(depth, b, 6, d)


def _proj_kernel(*refs, has_mod, n_extra, n_out, epilogue):
    x_ref = refs[0]
    pos = 1
    mod_ref = None
    if has_mod:
        mod_ref = refs[pos]
        pos += 1
    w_ref = refs[pos]
    pos += 1
    extra = refs[pos:pos + n_extra]
    pos += n_extra
    outs = refs[pos:pos + n_out]
    pos += n_out
    if has_mod:
        h_ref = refs[pos]

        @pl.when(pl.program_id(1) == 0)
        def _():
            mod = mod_ref[...]
            h_ref[...] = (x_ref[...] * (1.0 + mod[1:2, :]) + mod[0:1, :]).astype(BF16)

        h = h_ref[...]
    else:
        h = x_ref[...]
    acc = jnp.dot(h, w_ref[...], preferred_element_type=F32)
    epilogue(acc, pl.program_id(1), extra, outs)


def _proj(x, w, *, tm, tn, mod, seq, extras, outs, epilogue, vmem_mib=48, name):
    t, k = x.shape
    n = w.shape[1]
    in_specs = [pl.BlockSpec((tm, k), lambda i, j: (i, 0))]
    args = [x]
    if mod is not None:
        blocks_per_seq = seq // tm
        in_specs.append(pl.BlockSpec((None, 6, k), lambda i, j: (i // blocks_per_seq, 0, 0)))
        args.append(mod)
    in_specs.append(pl.BlockSpec((k, tn), lambda i, j: (0, j)))
    args.append(w)
    for arr, blk, imap in extras:
        in_specs.append(pl.BlockSpec(blk, imap))
        args.append(arr)
    out_specs = [pl.BlockSpec(blk, imap) for _, _, blk, imap in outs]
    out_shape = [jax.ShapeDtypeStruct(shp, dt) for shp, dt, _, _ in outs]
    scratch = [pltpu.VMEM((tm, k), BF16)] if mod is not None else []
    kern = functools.partial(_proj_kernel, has_mod=mod is not None, n_extra=len(extras),
                             n_out=len(outs), epilogue=epilogue)
    return pl.pallas_call(
        kern,
        grid=(t // tm, n // tn),
        in_specs=in_specs,
        out_specs=out_specs,
        out_shape=out_shape,
        scratch_shapes=scratch,
        compiler_params=_cparams(("parallel", "arbitrary"), vmem_mib),
        name=name,
    )(*args)


def _rope_tables(seq, dim):
    half = dim // 2
    inv_freq = ROPE_THETA ** (-jnp.arange(half, dtype=F32) * (2.0 / dim))
    ang = jnp.arange(seq, dtype=F32)[:, None] * inv_freq[None, :]
    return jnp.cos(ang), jnp.sin(ang)


def _mla_rope_tables(seq, scale):
    cos, sin = _rope_tables(seq, MLA_ROPE_DIM)
    z32 = jnp.zeros_like(cos)
    z64 = jnp.zeros((seq, LANES - MLA_ROPE_DIM), F32)
    c = jnp.concatenate([cos, cos, z64], axis=1) * scale
    s_left = jnp.concatenate([-sin, z32, z64], axis=1) * scale
    s_right = jnp.concatenate([z32, sin, z64], axis=1) * scale
    return c, s_left, s_right


def _rope64_in_chunk(r, c, s_left, s_right):
    half = MLA_ROPE_DIM // 2
    return (r * c + pltpu.roll(r, LANES - half, 1) * s_left + pltpu.roll(r, half, 1) * s_right)


def _mla_lat_epilogue(acc, j, extra, outs, *, q_rank, kv_rank):
    gq_ref, gkv_ref, c_ref, sl_ref, sr_ref = extra
    cq_ref, ckv_ref, kr_ref = outs

    def rms(v, g):
        return v * lax.rsqrt(jnp.mean(v * v, axis=-1, keepdims=True) + RMS_EPS) * g

    cq_ref[...] = rms(acc[:, :q_rank], gq_ref[...]).astype(BF16)
    ckv_ref[...] = rms(acc[:, q_rank:q_rank + kv_rank], gkv_ref[...]).astype(BF16)
    kr = acc[:, q_rank + kv_rank:]
    kr_ref[...] = _rope64_in_chunk(kr, c_ref[...], sl_ref[...], sr_ref[...]).astype(BF16)


def _mla_q_epilogue(acc, j, extra, outs, *, heads_per_tile, scale):
    c_ref, sl_ref, sr_ref = extra
    (q_ref,) = outs
    c, sl, sr = c_ref[...], sl_ref[...], sr_ref[...]
    for h in range(heads_per_tile):
        lo = h * 2 * LANES
        q_ref[:, lo:lo + LANES] = (acc[:, lo:lo + LANES] * scale).astype(BF16)
        q_ref[:, lo + LANES:lo + 2 * LANES] = _rope64_in_chunk(
            acc[:, lo + LANES:lo + 2 * LANES], c, sl, sr).astype(BF16)


def _mla_kv_kernel(ckv_ref, wkt_ref, wv_ref, kr_ref, kt_ref, v_ref, *, heads_per_tile):
    x = ckv_ref[...]
    kn_t = _dot_nt(wkt_ref[...], x)
    vv = jnp.dot(x, wv_ref[...], preferred_element_type=F32)
    kr_t = kr_ref[...].astype(F32).T.astype(BF16)
    ones = jnp.ones((x.shape[0], LANES), BF16)
    for h in range(heads_per_tile):
        lo = h * 2 * LANES
        kt_ref[lo:lo + LANES, :] = kn_t[h * LANES:(h + 1) * LANES, :].astype(BF16)
        kt_ref[lo + LANES:lo + 2 * LANES, :] = kr_t
        v_ref[:, lo:lo + LANES] = vv[:, h * LANES:(h + 1) * LANES].astype(BF16)
        v_ref[:, lo + LANES:lo + 2 * LANES] = ones


def _causal_mask(tq, tk):
    row = lax.broadcasted_iota(jnp.int32, (tq, tk), 0)
    col = lax.broadcasted_iota(jnp.int32, (tq, tk), 1)
    return col <= row


def _dot_nt(a, b):
    return lax.dot_general(a, b, (((1,), (1,)), ((), ())), preferred_element_type=F32)


def _qk(q, k_t):
    return jnp.dot(q, k_t, preferred_element_type=F32)


def _softmax_step(s, v, m_ref, l_ref, acc_ref):
    m_prev = m_ref[...]
    m_new = jnp.maximum(m_prev, jnp.max(s, axis=-1, keepdims=True))
    p = jnp.exp2(s - jnp.tile(m_new, (1, s.shape[1] // LANES)))
    alpha = jnp.exp2(m_prev - m_new)
    if l_ref is not None:
        l_ref[...] = alpha * l_ref[...] + jnp.sum(p, axis=-1, keepdims=True)
    acc_ref[...] = (acc_ref[...] * jnp.tile(alpha, (1, acc_ref.shape[1] // LANES))
                    + jnp.dot(p.astype(BF16), v, preferred_element_type=F32))
    m_ref[...] = m_new


def _mla_attn_kernel(q_ref, k_ref, v_ref, o_ref, m_ref, acc_ref, *, blk, seq, heads):
    w = 2 * LANES

    def q_body(qi, carry):
        q0 = pl.multiple_of(qi * blk, blk)
        qs = [q_ref[pl.ds(q0, blk), h * w:(h + 1) * w] for h in range(heads)]
        m_ref[...] = jnp.full(m_ref.shape, NEG_INF, F32)
        acc_ref[...] = jnp.zeros(acc_ref.shape, F32)

        def step(k0, masked):
            scores = [_qk(qs[h], k_ref[h * w:(h + 1) * w, pl.ds(k0, blk)]) for h in range(heads)]
            for h in range(heads):
                s = scores[h]
                if masked:
                    s = jnp.where(_causal_mask(blk, blk), s, NEG_INF)
                _softmax_step(s, v_ref[pl.ds(k0, blk), h * w:(h + 1) * w], m_ref.at[h], None,
                              acc_ref.at[h])

        def kv_body(kj, c):
            step(pl.multiple_of(kj * blk, blk), False)
            return c

        lax.fori_loop(0, qi, kv_body, 0)
        step(q0, True)
        for h in range(heads):
            acc = acc_ref[h]
            o_ref[pl.ds(q0, blk), h * HEAD_DIM:(h + 1) * HEAD_DIM] = (
                acc[:, :HEAD_DIM] / acc[:, HEAD_DIM:]).astype(o_ref.dtype)
        return carry

    lax.fori_loop(0, seq // blk, q_body, 0)


def _diff_attn_kernel(q_ref, k_ref, v_ref, lam_ref, g_ref, o_ref,
                      m1_ref, l1_ref, a1_ref, m2_ref, l2_ref, a2_ref, *, blk, seq, lam_init):
    hd = HEAD_DIM
    lf = lam_ref[...]
    lam_full = (jnp.exp(jnp.sum(lf[0:1, :] * lf[1:2, :], axis=-1, keepdims=True))
                - jnp.exp(jnp.sum(lf[2:3, :] * lf[3:4, :], axis=-1, keepdims=True)) + lam_init)

    def q_body(qi, carry):
        q0 = pl.multiple_of(qi * blk, blk)
        q = q_ref[pl.ds(q0, blk), :]
        q1, q2 = q[:, :hd], q[:, hd:]
        for m_ref, l_ref, a_ref in ((m1_ref, l1_ref, a1_ref), (m2_ref, l2_ref, a2_ref)):
            m_ref[...] = jnp.full(m_ref.shape, NEG_INF, F32)
            l_ref[...] = jnp.zeros(l_ref.shape, F32)
            a_ref[...] = jnp.zeros(a_ref.shape, F32)

        def step(k0, masked):
            v = v_ref[pl.ds(k0, blk), :]
            s1 = _qk(q1, k_ref[:hd, pl.ds(k0, blk)])
            s2 = _qk(q2, k_ref[hd:, pl.ds(k0, blk)])
            if masked:
                mask = _causal_mask(blk, blk)
                s1 = jnp.where(mask, s1, NEG_INF)
                s2 = jnp.where(mask, s2, NEG_INF)
            _softmax_step(s1, v, m1_ref, l1_ref, a1_ref)
            _softmax_step(s2, v, m2_ref, l2_ref, a2_ref)

        def kv_body(kj, c):
            step(pl.multiple_of(kj * blk, blk), False)
            return c

        lax.fori_loop(0, qi, kv_body, 0)
        step(q0, True)
        reps = (1, 2 * hd // LANES)
        o = (a1_ref[...] / jnp.tile(l1_ref[...], reps)
             - lam_full * (a2_ref[...] / jnp.tile(l2_ref[...], reps)))
        o = o * lax.rsqrt(jnp.mean(o * o, axis=-1, keepdims=True) + LN_EPS) * g_ref[...]
        o_ref[pl.ds(q0, blk), :] = (o * (1.0 - lam_init)).astype(o_ref.dtype)
        return carry

    lax.fori_loop(0, seq // blk, q_body, 0)


def _sb_attn_kernel(q_ref, k_ref, v_ref, o_ref, run_ref, acc_ref, *, tq, tk, seq, heads):
    chunks_per_q = tq // tk
    hd = HEAD_DIM
    jr = lax.broadcasted_iota(jnp.int32, (2 * tk, tk), 0)
    jc = lax.broadcasted_iota(jnp.int32, (2 * tk, tk), 1)
    later_mat = (jnp.where(jr >= tk, jr - tk, jr) > jc).astype(BF16)

    def q_body(qi, carry):
        q0 = pl.multiple_of(qi * tq, tq)
        qs = [q_ref[pl.ds(q0, tq), h * hd:(h + 1) * hd] for h in range(heads)]
        run_ref[...] = jnp.zeros(run_ref.shape, F32)
        acc_ref[...] = jnp.zeros(acc_ref.shape, F32)

        def step(k0, diag_offset):
            logits = [_qk(qs[h], k_ref[h * hd:(h + 1) * hd, pl.ds(k0, tk)]) for h in range(heads)]
            for h in range(heads):
                z = logits[h]
                log_beta = jnp.minimum(z, 0.0) - jnp.log2(1.0 + jnp.exp2(-jnp.abs(z)))
                log_keep = log_beta - z
                if diag_offset is not None:
                    row = lax.broadcasted_iota(jnp.int32, (tq, tk), 0)
                    col = lax.broadcasted_iota(jnp.int32, (tq, tk), 1) + diag_offset
                    strict = col < row
                    log_keep = jnp.where(strict, log_keep, 0.0)
                hi = log_keep.astype(BF16)
                lo = (log_keep - hi.astype(F32)).astype(BF16)
                run = run_ref[h]
                later = (jnp.dot(jnp.concatenate([hi, lo], axis=1), later_mat,
                                 preferred_element_type=F32) + jnp.tile(run, (1, tk // LANES)))
                a = jnp.exp2(log_beta + later)
                if diag_offset is not None:
                    a = jnp.where(strict, a, 0.0)
                acc_ref[h] += jnp.dot(a.astype(BF16), v_ref[pl.ds(k0, tk), h * hd:(h + 1) * hd],
                                      preferred_element_type=F32)
                run_ref[h] = run + jnp.sum(log_keep, axis=-1, keepdims=True)

        for c in reversed(range(chunks_per_q)):
            step(q0 + c * tk, c * tk)

        def any_live():
            run_max = run_ref[0]
            for h in range(1, heads):
                run_max = jnp.maximum(run_max, run_ref[h])
            return jnp.max(run_max) > SB_DEAD_LOG2

        def kv_cond(carry):
            n, live = carry
            return jnp.logical_and(n < qi * chunks_per_q, live)

        def kv_body(carry):
            n, _ = carry
            kj = qi * chunks_per_q - 1 - n
            step(pl.multiple_of(kj * tk, tk), None)
            return n + 1, any_live()

        lax.while_loop(kv_cond, kv_body, (jnp.int32(0), any_live()))
        for h in range(heads):
            o_ref[pl.ds(q0, tq), h * hd:(h + 1) * hd] = acc_ref[h].astype(o_ref.dtype)
        return carry

    lax.fori_loop(0, seq // tq, q_body, 0)


def _attention(kernel, q, k, v, *, batch, seq, groups, k_off, v_off, dq, dv, d_out, scratch,
               vmem_mib, name):
    in_specs = [
        pl.BlockSpec((seq, dq), lambda b, g: (b, g)),
        pl.BlockSpec((dq, seq), lambda b, g: (k_off + g, b)),
        pl.BlockSpec((seq, dv), lambda b, g: (b, v_off + g)),
    ]
    return pl.pallas_call(
        kernel,
        grid=(batch, groups),
        in_specs=in_specs,
        out_specs=pl.BlockSpec((seq, d_out), lambda b, g: (b, g)),
        out_shape=jax.ShapeDtypeStruct((batch * seq, groups * d_out), BF16),
        scratch_shapes=scratch,
        compiler_params=_cparams(("parallel", "parallel"), vmem_mib),
        name=name,
    )(q, k, v)


def _mla_mixer(x2d, mod, seq, batch, w_in, q_norm_g, kv_norm_g, w_uq, w_ukv):
    t, d = x2d.shape
    q_rank, kv_rank = q_norm_g.shape[0], kv_norm_g.shape[0]
    heads = w_uq.shape[1] // (HEAD_DIM + MLA_ROPE_DIM)
    scale = (HEAD_DIM + MLA_ROPE_DIM) ** -0.5 * LOG2_E
    tm = min(512, seq)
    row_blocks = seq // tm

    lat_n = q_rank + kv_rank + LANES
    w_in_p = jnp.zeros((d, lat_n), BF16).at[:, :w_in.shape[1]].set(w_in.astype(BF16))
    c_k, sl_k, sr_k = _mla_rope_tables(seq, 1.0)
    c_q, sl_q, sr_q = _mla_rope_tables(seq, scale)
    tab_spec = ((tm, LANES), lambda i, j: (i % row_blocks, 0))

    cq, ckv, kr = _proj(
        x2d, w_in_p, tm=tm, tn=lat_n, mod=mod, seq=seq,
        extras=[(q_norm_g.reshape(1, q_rank), (1, q_rank), lambda i, j: (0, 0)),
                (kv_norm_g.reshape(1, kv_rank), (1, kv_rank), lambda i, j: (0, 0)),
                (c_k,) + tab_spec, (sl_k,) + tab_spec, (sr_k,) + tab_spec],
        outs=[((t, q_rank), BF16, (tm, q_rank), lambda i, j: (i, 0)),
              ((t, kv_rank), BF16, (tm, kv_rank), lambda i, j: (i, 0)),
              ((t, LANES), BF16, (tm, LANES), lambda i, j: (i, 0))],
        epilogue=functools.partial(_mla_lat_epilogue, q_rank=q_rank, kv_rank=kv_rank),
        name="mla_latent")

    w_q = w_uq.reshape(q_rank, heads, HEAD_DIM + MLA_ROPE_DIM).astype(BF16)
    w_q = jnp.pad(w_q, ((0, 0), (0, 0), (0, 2 * LANES - HEAD_DIM - MLA_ROPE_DIM)))
    w_q = w_q.reshape(q_rank, heads * 2 * LANES)
    hpt = min(heads, 4)
    (q_cat,) = _proj(
        cq, w_q, tm=tm, tn=hpt * 2 * LANES, mod=None, seq=seq,
        extras=[(c_q,) + tab_spec, (sl_q,) + tab_spec, (sr_q,) + tab_spec],
        outs=[((t, heads * 2 * LANES), BF16, (tm, hpt * 2 * LANES), lambda i, j: (i, j))],
        epilogue=functools.partial(_mla_q_epilogue, heads_per_tile=hpt, scale=scale),
        name="mla_q_up")

    w_kv = w_ukv.reshape(kv_rank, heads, 2 * HEAD_DIM).astype(BF16)
    w_kt = w_kv[:, :, :HEAD_DIM].reshape(kv_rank, heads * HEAD_DIM).T
    w_v = w_kv[:, :, HEAD_DIM:].reshape(kv_rank, heads * HEAD_DIM)
    k_cat, v = pl.pallas_call(
        functools.partial(_mla_kv_kernel, heads_per_tile=hpt),
        grid=(t // tm, heads // hpt),
        in_specs=[
            pl.BlockSpec((tm, kv_rank), lambda i, j: (i, 0)),
            pl.BlockSpec((hpt * HEAD_DIM, kv_rank), lambda i, j: (j, 0)),
            pl.BlockSpec((kv_rank, hpt * HEAD_DIM), lambda i, j: (0, j)),
            pl.BlockSpec((tm, LANES), lambda i, j: (i, 0)),
        ],
        out_specs=[
            pl.BlockSpec((hpt * 2 * LANES, tm), lambda i, j: (j, i)),
            pl.BlockSpec((tm, hpt * 2 * LANES), lambda i, j: (i, j)),
        ],
        out_shape=[jax.ShapeDtypeStruct((heads * 2 * LANES, t), BF16),
                   jax.ShapeDtypeStruct((t, heads * 2 * LANES), BF16)],
        compiler_params=_cparams(("parallel", "parallel"), 32),
        name="mla_kv_up",
    )(ckv, w_kt, w_v, kr)

    blk = min(512, seq)
    hg = ATTN_HEAD_GROUP
    return _attention(
        functools.partial(_mla_attn_kernel, blk=blk, seq=seq, heads=hg), q_cat, k_cat, v,
        batch=batch, seq=seq, groups=heads // hg, k_off=0, v_off=0,
        dq=hg * 2 * LANES, dv=hg * 2 * LANES, d_out=hg * HEAD_DIM,
        scratch=[pltpu.VMEM((hg, blk, LANES), F32), pltpu.VMEM((hg, blk, 2 * LANES), F32)],
        vmem_mib=48, name="mla_attention")


def _qkv_out_specs(t, width, tm, tn):
    nq = width // tn

    def qv_map(i, j):
        return i, jnp.where(j < nq, j, jnp.where(j < 2 * nq, nq - 1, j - nq))

    def kt_map(i, j):
        return jnp.clip(j - nq, 0, nq - 1), i

    return [((t, 2 * width), BF16, (tm, tn), qv_map), ((width, t), BF16, (tn, tm), kt_map)]


def _diff_qkv_epilogue(acc, j, extra, outs, *, tn, width, scale):
    cos_ref, sin_ref = extra
    qv_ref, kt_ref = outs

    def rope(factor):
        cos = cos_ref[...] * factor
        sin = sin_ref[...] * factor
        chunks = []
        for c in range(tn // LANES):
            a = acc[:, c * LANES:(c + 1) * LANES]
            chunks.append(a * cos + pltpu.roll(a, HEAD_DIM // 2, 1) * sin)
        return jnp.concatenate(chunks, axis=1)

    @pl.when(j * tn < width)
    def _():
        qv_ref[...] = rope(scale).astype(BF16)

    @pl.when(jnp.logical_and(j * tn >= width, j * tn < 2 * width))
    def _():
        kt_ref[...] = rope(1.0).T.astype(BF16)

    @pl.when(j * tn >= 2 * width)
    def _():
        qv_ref[...] = acc.astype(BF16)


def _diff_mixer(x2d, mod, seq, batch, layer_idx, w_qkv, lam, subln_g):
    t, d = x2d.shape
    width = w_qkv.shape[1] // 3
    heads = width // (2 * HEAD_DIM)
    scale = HEAD_DIM ** -0.5 * LOG2_E
    lam_init = 0.8 - 0.6 * math.exp(-0.3 * layer_idx)
    tm = min(512, seq)
    tn = min(1024, width)
    row_blocks = seq // tm
    cos, sin = _rope_tables(seq, HEAD_DIM)
    cos_t = jnp.concatenate([cos, cos], axis=1)
    sin_t = jnp.concatenate([-sin, sin], axis=1)
    tab_spec = ((tm, LANES), lambda i, j: (i % row_blocks, 0))
    qv, k_t = _proj(
        x2d, w_qkv.astype(BF16), tm=tm, tn=tn, mod=mod, seq=seq,
        extras=[(cos_t,) + tab_spec, (sin_t,) + tab_spec],
        outs=_qkv_out_specs(t, width, tm, tn),
        epilogue=functools.partial(_diff_qkv_epilogue, tn=tn, width=width, scale=scale),
        name="diff_qkv")
    blk = min(512, seq)
    dh = 2 * HEAD_DIM
    in_specs = [
        pl.BlockSpec((seq, dh), lambda b, h: (b, h)),
        pl.BlockSpec((dh, seq), lambda b, h: (h, b)),
        pl.BlockSpec((seq, dh), lambda b, h: (b, heads + h)),
        pl.BlockSpec((4, HEAD_DIM), lambda b, h: (0, 0)),
        pl.BlockSpec((1, dh), lambda b, h: (0, 0)),
    ]
    stat = lambda: pltpu.VMEM((blk, LANES), F32)
    accum = lambda: pltpu.VMEM((blk, dh), F32)
    return pl.pallas_call(
        functools.partial(_diff_attn_kernel, blk=blk, seq=seq, lam_init=lam_init),
        grid=(batch, heads),
        in_specs=in_specs,
        out_specs=pl.BlockSpec((seq, dh), lambda b, h: (b, h)),
        out_shape=jax.ShapeDtypeStruct((t, width), BF16),
        scratch_shapes=[stat(), stat(), accum(), stat(), stat(), accum()],
        compiler_params=_cparams(("parallel", "parallel"), 40),
        name="diff_attention",
    )(qv, k_t, qv, lam, subln_g.reshape(1, dh))


def _sb_qkv_epilogue(acc, j, extra, outs, *, tn, width, scale):
    qv_ref, kt_ref = outs

    @pl.when(j * tn < width)
    def _():
        qv_ref[...] = (acc * scale).astype(BF16)

    @pl.when(jnp.logical_and(j * tn >= width, j * tn < 2 * width))
    def _():
        kt_ref[...] = acc.T.astype(BF16)

    @pl.when(j * tn >= 2 * width)
    def _():
        qv_ref[...] = acc.astype(BF16)


def _sb_mixer(x2d, mod, seq, batch, w_qkv):
    t, d = x2d.shape
    width = w_qkv.shape[1] // 3
    heads = width // HEAD_DIM
    scale = HEAD_DIM ** -0.5 * LOG2_E
    tm = min(512, seq)
    tn = min(1024, width)
    qv, k_t = _proj(
        x2d, w_qkv.astype(BF16), tm=tm, tn=tn, mod=mod, seq=seq, extras=[],
        outs=_qkv_out_specs(t, width, tm, tn),
        epilogue=functools.partial(_sb_qkv_epilogue, tn=tn, width=width, scale=scale),
        name="sb_qkv")
    tq = min(512, seq)
    tk = min(256, seq)
    hg = ATTN_HEAD_GROUP
    groups = heads // hg
    return _attention(
        functools.partial(_sb_attn_kernel, tq=tq, tk=tk, seq=seq, heads=hg), qv, k_t, qv,
        batch=batch, seq=seq, groups=groups, k_off=0, v_off=groups,
        dq=hg * HEAD_DIM, dv=hg * HEAD_DIM, d_out=hg * HEAD_DIM,
        scratch=[pltpu.VMEM((hg, tq, LANES), F32), pltpu.VMEM((hg, tq, HEAD_DIM), F32)],
        vmem_mib=40, name="sb_attention")


def _split_bf16(v):
    hi = v.astype(BF16)
    return hi, (v - hi.astype(F32)).astype(BF16)


def _outproj_router_kernel(o_ref, w_ref, x_ref, mod_ref, g_ref, b_ref, rw_ref, rb_ref,
                           x1_ref, idx_ref, gate_ref, *, alpha, n_experts):
    y = jnp.dot(o_ref[...], w_ref[...], preferred_element_type=F32)
    mod = mod_ref[...]
    x1 = _layer_norm_rows(alpha * x_ref[...] + mod[2:3, :] * y, g_ref[...], b_ref[...])
    x1_ref[...] = x1
    h2 = x1 * (1.0 + mod[4:5, :]) + mod[3:4, :]
    hi, lo = _split_bf16(h2)
    logits = jnp.dot(jnp.concatenate([hi, lo, hi], axis=1), rw_ref[...],
                     preferred_element_type=F32) + rb_ref[...]
    lane = lax.broadcasted_iota(jnp.int32, logits.shape, 1).astype(F32)
    work = jnp.where(lane < n_experts, logits, -jnp.inf)
    idx_out = jnp.zeros(logits.shape, F32)
    val_out = jnp.zeros(logits.shape, F32)
    top = None
    denom = None
    for kk in range(TOP_K):
        m = jnp.max(work, axis=-1, keepdims=True)
        first = jnp.min(jnp.where(work == m, lane, float(LANES)), axis=-1, keepdims=True)
        if kk == 0:
            top = m
        e = jnp.exp(m - top)
        denom = e if kk == 0 else denom + e
        idx_out = jnp.where(lane == kk, first, idx_out)
        val_out = jnp.where(lane == kk, e, val_out)
        work = jnp.where(lane == first, -jnp.inf, work)
    idx_ref[...] = idx_out.astype(jnp.int32)
    gate_ref[...] = val_out / denom


def _outproj_router(o, w_o, x2d, mod, ln_g, ln_b, router_w, router_b, *, seq, alpha):
    t, d = x2d.shape
    n_experts = router_w.shape[1]
    tm = min(256, seq)
    blocks_per_seq = seq // tm
    rw = jnp.zeros((d, LANES), F32).at[:, :n_experts].set(router_w)
    rw_hi, rw_lo = _split_bf16(rw)
    rw3 = jnp.concatenate([rw_hi, rw_hi, rw_lo], axis=0)
    rb = jnp.zeros((1, LANES), F32).at[0, :n_experts].set(router_b)
    row = lambda i: (i, 0)
    fixed = lambda i: (0, 0)
    return pl.pallas_call(
        functools.partial(_outproj_router_kernel, alpha=alpha, n_experts=n_experts),
        grid=(t // tm,),
        in_specs=[
            pl.BlockSpec((tm, o.shape[1]), row),
            pl.BlockSpec(w_o.shape, fixed),
            pl.BlockSpec((tm, d), row),
            pl.BlockSpec((None, 6, d), lambda i: (i // blocks_per_seq, 0, 0)),
            pl.BlockSpec((1, d), fixed),
            pl.BlockSpec((1, d), fixed),
            pl.BlockSpec((3 * d, LANES), fixed),
            pl.BlockSpec((1, LANES), fixed),
        ],
        out_specs=[pl.BlockSpec((tm, d), row),
                   pl.BlockSpec((tm, LANES), row), pl.BlockSpec((tm, LANES), row)],
        out_shape=[jax.ShapeDtypeStruct((t, d), F32),
                   jax.ShapeDtypeStruct((t, LANES), jnp.int32), jax.ShapeDtypeStruct((t, LANES), F32)],
        compiler_params=_cparams(("parallel",), 48),
        name="outproj_ln_router",
    )(o, w_o.astype(BF16), x2d, mod, ln_g.reshape(1, d), ln_b.reshape(1, d), rw3, rb)


def _plan_kernel(idx_ref, rank_ref, cnt_ref, run_ref):
    @pl.when(pl.program_id(0) == 0)
    def _():
        run_ref[...] = jnp.zeros(run_ref.shape, F32)

    idx = idx_ref[...]
    tb = idx.shape[0]
    lane = lax.broadcasted_iota(jnp.int32, idx.shape, 1)
    onehots = [(lane == idx[:, kk:kk + 1]).astype(F32) for kk in range(TOP_K)]
    chosen = onehots[0]
    for kk in range(1, TOP_K):
        chosen = chosen + onehots[kk]
    earlier = (lax.broadcasted_iota(jnp.int32, (tb, tb), 1)
               < lax.broadcasted_iota(jnp.int32, (tb, tb), 0)).astype(BF16)
    prefix = jnp.dot(earlier, chosen.astype(BF16), preferred_element_type=F32) + run_ref[0:1, :]
    rank = jnp.zeros(idx.shape, F32)
    for kk in range(TOP_K):
        rank = jnp.where(lane == kk, jnp.sum(prefix * onehots[kk], axis=-1, keepdims=True), rank)
    rank_ref[...] = rank.astype(jnp.int32)
    run_ref[...] = run_ref[...] + jnp.sum(chosen, axis=0, keepdims=True)
    cnt_ref[...] = run_ref[...]


def _routing_plan(idx_pad, n_experts, tm):
    t = idx_pad.shape[0]
    tb = min(512, t)
    rank_pad, cnt = pl.pallas_call(
        _plan_kernel,
        grid=(t // tb,),
        in_specs=[pl.BlockSpec((tb, LANES), lambda i: (i, 0))],
        out_specs=[pl.BlockSpec((tb, LANES), lambda i: (i, 0)),
                   pl.BlockSpec((SUBLANES, LANES), lambda i: (0, 0))],
        out_shape=[jax.ShapeDtypeStruct((t, LANES), jnp.int32),
                   jax.ShapeDtypeStruct((SUBLANES, LANES), F32)],
        scratch_shapes=[pltpu.VMEM((SUBLANES, LANES), F32)],
        compiler_params=_cparams(("arbitrary",), 16),
        name="moe_plan",
    )(idx_pad)
    counts = cnt[0, :n_experts].astype(jnp.int32)
    padded = (counts + tm - 1) // tm * tm
    pad_end = jnp.cumsum(padded)
    pad_start = pad_end - padded
    chose = idx_pad[:, :TOP_K, None] == jnp.arange(n_experts, dtype=jnp.int32)
    dest = (jnp.sum(jnp.where(chose, pad_start, 0), axis=-1) + rank_pad[:, :TOP_K]).astype(jnp.int32)
    n_blocks = t * TOP_K // tm + n_experts
    block_start = jnp.arange(n_blocks, dtype=jnp.int32) * tm
    block_expert = jnp.minimum(jnp.searchsorted(pad_end, block_start, side='right'),
                               n_experts - 1).astype(jnp.int32)
    n_used = (pad_end[-1] // tm).astype(jnp.int32).reshape(1)
    fill_start = jnp.concatenate([pad_start + counts, pad_end[-1:]]).astype(jnp.int32)
    fill_count = jnp.concatenate([pad_end - pad_start - counts,
                                  n_blocks * tm - pad_end[-1:]]).astype(jnp.int32)
    return dest.reshape(-1), block_expert, n_used, fill_start, fill_count, n_blocks


def _load_rows(ref, lead, n_rows, chunks):
    return jnp.concatenate(
        [ref[lead + (pl.ds(c, n_rows, stride=chunks), slice(None))] for c in range(chunks)], axis=1)


def _store_rows(ref, lead, val, chunks):
    for c in range(chunks):
        ref[lead + (pl.ds(c, val.shape[0], stride=chunks), slice(None))] = val[:, c * LANES:(c + 1) * LANES]


def _dispatch_kernel(dest_ref, fstart_ref, fcount_ref, x_ref, mod_ref, xs_hbm, hbuf, zrow, sem, fsem,
                     *, tb, n_steps, n_fills, chunks):
    i = pl.program_id(0)
    slot = i % 2

    def drain(buf_slot):
        for _ in range(TOP_K):
            pltpu.make_async_copy(hbuf.at[buf_slot], xs_hbm.at[pl.ds(0, tb * chunks)],
                                  sem.at[buf_slot]).wait()

    def for_each_fill(fn):
        def per_range(f, c):
            start = fstart_ref[f]

            def per_row(j, c2):
                fn(pltpu.make_async_copy(zrow, xs_hbm.at[pl.ds((start + j) * chunks, chunks)], fsem))
                return c2

            lax.fori_loop(0, fcount_ref[f], per_row, 0)
            return c

        lax.fori_loop(0, n_fills, per_range, 0)

    @pl.when(i == 0)
    def _():
        zrow[...] = jnp.zeros(zrow.shape, F32)
        for_each_fill(lambda copy: copy.start())

    @pl.when(i >= 2)
    def _():
        drain(slot)

    mod = mod_ref[...]
    h2 = x_ref[...] * (1.0 + mod[4:5, :]) + mod[3:4, :]
    _store_rows(hbuf, (slot,), h2, chunks)

    def per_token(r, c):
        base = (i * tb + r) * TOP_K
        src = hbuf.at[slot, pl.ds(r * chunks, chunks)]
        for kk in range(TOP_K):
            pltpu.make_async_copy(src, xs_hbm.at[pl.ds(dest_ref[base + kk] * chunks, chunks)],
                                  sem.at[slot]).start()
        return c

    lax.fori_loop(0, tb, per_token, 0, unroll=4)

    @pl.when(i == n_steps - 1)
    def _():
        drain(slot)
        if n_steps > 1:
            drain(1 - slot)
        for_each_fill(lambda copy: copy.wait())


def _dispatch(x1, mod, plan, *, seq, tm):
    dest, _, _, fill_start, fill_count, n_blocks = plan
    t, d = x1.shape
    chunks = d // LANES
    tb = min(256, seq)
    n_steps = t // tb
    blocks_per_seq = seq // tb
    grid_spec = pltpu.PrefetchScalarGridSpec(
        num_scalar_prefetch=3,
        grid=(n_steps,),
        in_specs=[
            pl.BlockSpec((tb, d), lambda i, *_: (i, 0)),
            pl.BlockSpec((None, 6, d), lambda i, *_: (i // blocks_per_seq, 0, 0)),
        ],
        out_specs=pl.BlockSpec(memory_space=pl.ANY),
        scratch_shapes=[pltpu.VMEM((2, tb * chunks, LANES), F32), pltpu.VMEM((chunks, LANES), F32),
                        pltpu.SemaphoreType.DMA((2,)), pltpu.SemaphoreType.DMA(())],
    )
    return pl.pallas_call(
        functools.partial(_dispatch_kernel, tb=tb, n_steps=n_steps, n_fills=fill_start.shape[0],
                          chunks=chunks),
        grid_spec=grid_spec,
        out_shape=jax.ShapeDtypeStruct((n_blocks * tm * chunks, LANES), F32),
        compiler_params=_cparams(("arbitrary",), 32),
        name="moe_dispatch",
    )(dest, fill_start, fill_count, x1, mod)


def _expert_kernel(be_ref, used_ref, xs_ref, wgu_ref, bgu_ref, wd_ref, bd_ref, y_ref,
                   *, tm, d_expert, chunks):
    i = pl.program_id(0)

    @pl.when(i < used_ref[0])
    def _():
        x = _load_rows(xs_ref, (), tm, chunks).astype(BF16)
        gu = jnp.dot(x, wgu_ref[...], preferred_element_type=F32) + bgu_ref[...]
        g = jnp.minimum(gu[:, :d_expert], SWIGLU_LIMIT)
        lin = jnp.clip(gu[:, d_expert:], -SWIGLU_LIMIT, SWIGLU_LIMIT)
        act = g * jax.nn.sigmoid(SWIGLU_ALPHA * g) * (lin + 1.0)
        y = jnp.dot(act.astype(BF16), wd_ref[...], preferred_element_type=F32) + bd_ref[...]
        _store_rows(y_ref, (), y, chunks)

    @pl.when(i >= used_ref[0])
    def _():
        y_ref[...] = jnp.zeros(y_ref.shape, F32)


def _experts(xs, plan, layer, w_gate_up, b_gate_up, w_down, b_down, tm):
    _, block_expert, n_used, _, _, n_blocks = plan
    depth, n_experts, d_expert, d = w_down.shape
    chunks = d // LANES
    per_expert = lambda i, be, used: (layer, be[i], 0, 0)
    rows = lambda i, be, used: (i, 0)
    used_rows = lambda i, be, used: (jnp.minimum(i, used[0] - 1), 0)
    grid_spec = pltpu.PrefetchScalarGridSpec(
        num_scalar_prefetch=2,
        grid=(n_blocks,),
        in_specs=[
            pl.BlockSpec((tm * chunks, LANES), used_rows),
            pl.BlockSpec((None, None, d, 2 * d_expert), per_expert),
            pl.BlockSpec((None, None, 1, 2 * d_expert), per_expert),
            pl.BlockSpec((None, None, d_expert, d), per_expert),
            pl.BlockSpec((None, None, 1, d), per_expert),
        ],
        out_specs=pl.BlockSpec((tm * chunks, LANES), rows),
    )
    return pl.pallas_call(
        functools.partial(_expert_kernel, tm=tm, d_expert=d_expert, chunks=chunks),
        grid_spec=grid_spec,
        out_shape=jax.ShapeDtypeStruct((n_blocks * tm * chunks, LANES), F32),
        compiler_params=_cparams(("arbitrary",), 48),
        name="moe_experts",
    )(block_expert, n_used, xs, w_gate_up, b_gate_up.reshape(depth, n_experts, 1, 2 * d_expert),
      w_down, b_down.reshape(depth, n_experts, 1, d))


def _combine_kernel(dest_ref, ys_hbm, x_ref, gate_ref, mod_ref, g_ref, b_ref, o_ref, ybuf, sem,
                    *, tn, n_blocks, alpha, chunks):
    i = pl.program_id(0)
    slot = i % 2

    def issue(block, buf_slot):
        def body(r, c):
            base = (block * tn + r) * TOP_K
            for kk in range(TOP_K):
                pltpu.make_async_copy(ys_hbm.at[pl.ds(dest_ref[base + kk] * chunks, chunks)],
                                      ybuf.at[buf_slot, kk, pl.ds(r * chunks, chunks)],
                                      sem.at[buf_slot]).start()
            return c
        lax.fori_loop(0, tn, body, 0, unroll=4)

    @pl.when(i == 0)
    def _():
        issue(0, 0)

    @pl.when(i + 1 < n_blocks)
    def _():
        issue(i + 1, 1 - slot)

    for kk in range(TOP_K):
        pltpu.make_async_copy(ys_hbm.at[pl.ds(0, tn * chunks)], ybuf.at[slot, kk], sem.at[slot]).wait()

    gate = gate_ref[...]
    y = None
    for kk in range(TOP_K):
        term = gate[:, kk:kk + 1] * _load_rows(ybuf, (slot, kk), tn, chunks)
        y = term if y is None else y + term
    mod = mod_ref[...]
    o_ref[...] = _layer_norm_rows(alpha * x_ref[...] + mod[5:6, :] * y, g_ref[...], b_ref[...])


def _combine(ys, dest, x1, gate_pad, mod, ln_g, ln_b, *, seq, alpha):
    t, d = x1.shape
    chunks = d // LANES
    tn = min(128, seq)
    n_blocks = t // tn
    blocks_per_seq = seq // tn
    grid_spec = pltpu.PrefetchScalarGridSpec(
        num_scalar_prefetch=1,
        grid=(n_blocks,),
        in_specs=[
            pl.BlockSpec(memory_space=pl.ANY),
            pl.BlockSpec((tn, d), lambda i, dst: (i, 0)),
            pl.BlockSpec((tn, LANES), lambda i, dst: (i, 0)),
            pl.BlockSpec((None, 6, d), lambda i, dst: (i // blocks_per_seq, 0, 0)),
            pl.BlockSpec((1, d), lambda i, dst: (0, 0)),
            pl.BlockSpec((1, d), lambda i, dst: (0, 0)),
        ],
        out_specs=pl.BlockSpec((tn, d), lambda i, dst: (i, 0)),
        scratch_shapes=[pltpu.VMEM((2, TOP_K, tn * chunks, LANES), F32), pltpu.SemaphoreType.DMA((2,))],
    )
    return pl.pallas_call(
        functools.partial(_combine_kernel, tn=tn, n_blocks=n_blocks, alpha=alpha, chunks=chunks),
        grid_spec=grid_spec,
        out_shape=jax.ShapeDtypeStruct((t, d), F32),
        compiler_params=_cparams(("arbitrary",), 40),
        name="moe_combine_ln",
    )(dest, ys, x1, gate_pad, mod, ln_g.reshape(1, d), ln_b.reshape(1, d))


def kernel(x, c, ada_w, ada_b, ln_g, ln_b, mla_w_in, mla_q_norm_g, mla_kv_norm_g, mla_w_uq, mla_w_ukv,
           mla_w_o, diff_w_qkv, diff_lambda, diff_subln_g, diff_w_o, sb_w_qkv, sb_w_o, moe_router_w,
           moe_router_b, moe_w_gate_up, moe_b_gate_up, moe_w_down, moe_b_down):
    batch, seq, d = x.shape
    depth = ada_w.shape[0]
    alpha = (2 * depth) ** 0.25
    n_experts = moe_router_w.shape[-1]
    moe_tm = 256
    mods = _adaln(c, ada_w, ada_b)
    w_gate_up = moe_w_gate_up.astype(BF16)
    w_down = moe_w_down.astype(BF16)
    x2d = x.reshape(batch * seq, d)
    for i in range(depth):
        mod = mods[i]
        kind, j = i % N_MIXERS, i // N_MIXERS
        if kind == 0:
            o = _mla_mixer(x2d, mod, seq, batch, mla_w_in[j], mla_q_norm_g[j], mla_kv_norm_g[j],
                           mla_w_uq[j], mla_w_ukv[j])
            w_o = mla_w_o[j]
        elif kind == 1:
            o = _diff_mixer(x2d, mod, seq, batch, i, diff_w_qkv[j], diff_lambda[j], diff_subln_g[j])
            w_o = diff_w_o[j]
        else:
            o = _sb_mixer(x2d, mod, seq, batch, sb_w_qkv[j])
            w_o = sb_w_o[j]
        x1, idx_pad, gate_pad = _outproj_router(
            o, w_o, x2d, mod, ln_g[i, 0], ln_b[i, 0], moe_router_w[i], moe_router_b[i],
            seq=seq, alpha=alpha)
        plan = _routing_plan(idx_pad, n_experts, moe_tm)
        xs = _dispatch(x1, mod, plan, seq=seq, tm=moe_tm)
        ys = _experts(xs, plan, i, w_gate_up, moe_b_gate_up, w_down, moe_b_down, moe_tm)
        x2d = _combine(ys, plan[0], x1, gate_pad, mod, ln_g[i, 1], ln_b[i, 1], seq=seq, alpha=alpha)
    return x2d.reshape(batch, seq, d)
```

```python
import functools
import math

import jax
import jax.numpy as jnp
from jax import lax
from jax.experimental import pallas as pl
from jax.experimental.pallas import tpu as pltpu

F32 = jnp.float32
BF16 = jnp.bfloat16

N_MIXERS = 3
NEG_INF = -1e30
ROPE_THETA = 10000.0
LN_EPS = 1e-5
RMS_EPS = 1e-6
HEAD_DIM = 128
MLA_ROPE_DIM = 64
TOP_K = 4
SWIGLU_LIMIT = 7.0
SWIGLU_ALPHA = 1.702
LANES = 128
SUBLANES = 8
LOG2_E = 1.4426950408889634
ATTN_HEAD_GROUP = 2
SB_DEAD_LOG2 = -160.0
MIB = 1024 * 1024


def _cparams(sems, vmem_mib):
    return pltpu.CompilerParams(dimension_semantics=sems, vmem_limit_bytes=vmem_mib * MIB)


def _layer_norm_rows(r, g, b):
    mu = jnp.mean(r, axis=-1, keepdims=True)
    d = r - mu
    var = jnp.mean(d * d, axis=-1, keepdims=True)
    return d * lax.rsqrt(var + LN_EPS) * g + b


def _adaln_kernel(c_ref, w_ref, b_ref, o_ref):
    c = c_ref[...]
    cond = c * jax.nn.sigmoid(c)
    o_ref[...] = jnp.dot(cond.astype(BF16), w_ref[...].astype(BF16),
                         preferred_element_type=F32) + b_ref[...]


def _adaln(c, ada_w, ada_b):
    depth, d, n = ada_w.shape
    b = c.shape[0]
    rows = 16
    tn = next(cand for cand in (1024, 512, 256, 128) if n % cand == 0)
    c_pad = jnp.zeros((rows, d), F32).at[:b].set(c)
    out = pl.pallas_call(
        _adaln_kernel,
        grid=(depth, n // tn),
        in_specs=[
            pl.BlockSpec((rows, d), lambda l, j: (0, 0)),
            pl.BlockSpec((None, d, tn), lambda l, j: (l, 0, j)),
            pl.BlockSpec((None, 1, tn), lambda l, j: (l, 0, j)),
        ],
        out_specs=pl.BlockSpec((None, rows, tn), lambda l, j: (l, 0, j)),
        out_shape=jax.ShapeDtypeStruct((depth, rows, n), F32),
        compiler_params=_cparams(("parallel", "parallel"), 40),
        name="adaln",
    )(c_pad, ada_w, ada_b.reshape(depth, 1, n))
    return out[:, :b].reshape(depth, b, 6, d)


def _proj_kernel(*refs, has_mod, n_extra, n_out, epilogue):
    x_ref = refs[0]
    pos = 1
    mod_ref = None
    if has_mod:
        mod_ref = refs[pos]
        pos += 1
    w_ref = refs[pos]
    pos += 1
    extra = refs[pos:pos + n_extra]
    pos += n_extra
    outs = refs[pos:pos + n_out]
    pos += n_out
    if has_mod:
        h_ref = refs[pos]

        @pl.when(pl.program_id(1) == 0)
        def _():
            mod = mod_ref[...]
            h_ref[...] = (x_ref[...] * (1.0 + mod[1:2, :]) + mod[0:1, :]).astype(BF16)

        h = h_ref[...]
    else:
        h = x_ref[...]
    acc = jnp.dot(h, w_ref[...], preferred_element_type=F32)
    epilogue(acc, pl.program_id(1), extra, outs)


def _proj(x, w, *, tm, tn, mod, seq, extras, outs, epilogue, vmem_mib=48, name):
    t, k = x.shape
    n = w.shape[1]
    in_specs = [pl.BlockSpec((tm, k), lambda i, j: (i, 0))]
    args = [x]
    if mod is not None:
        blocks_per_seq = seq // tm
        in_specs.append(pl.BlockSpec((None, 6, k), lambda i, j: (i // blocks_per_seq, 0, 0)))
        args.append(mod)
    in_specs.append(pl.BlockSpec((k, tn), lambda i, j: (0, j)))
    args.append(w)
    for arr, blk, imap in extras:
        in_specs.append(pl.BlockSpec(blk, imap))
        args.append(arr)
    out_specs = [pl.BlockSpec(blk, imap) for _, _, blk, imap in outs]
    out_shape = [jax.ShapeDtypeStruct(shp, dt) for shp, dt, _, _ in outs]
    scratch = [pltpu.VMEM((tm, k), BF16)] if mod is not None else []
    kern = functools.partial(_proj_kernel, has_mod=mod is not None, n_extra=len(extras),
                             n_out=len(outs), epilogue=epilogue)
    return pl.pallas_call(
        kern,
        grid=(t // tm, n // tn),
        in_specs=in_specs,
        out_specs=out_specs,
        out_shape=out_shape,
        scratch_shapes=scratch,
        compiler_params=_cparams(("parallel", "arbitrary"), vmem_mib),
        name=name,
    )(*args)


def _rope_tables(seq, dim):
    half = dim // 2
    inv_freq = ROPE_THETA ** (-jnp.arange(half, dtype=F32) * (2.0 / dim))
    ang = jnp.arange(seq, dtype=F32)[:, None] * inv_freq[None, :]
    return jnp.cos(ang), jnp.sin(ang)


def _mla_rope_tables(seq, scale):
    cos, sin = _rope_tables(seq, MLA_ROPE_DIM)
    z32 = jnp.zeros_like(cos)
    z64 = jnp.zeros((seq, LANES - MLA_ROPE_DIM), F32)
    c = jnp.concatenate([cos, cos, z64], axis=1) * scale
    s_left = jnp.concatenate([-sin, z32, z64], axis=1) * scale
    s_right = jnp.concatenate([z32, sin, z64], axis=1) * scale
    return c, s_left, s_right


def _rope64_in_chunk(r, c, s_left, s_right):
    half = MLA_ROPE_DIM // 2
    return (r * c + pltpu.roll(r, LANES - half, 1) * s_left + pltpu.roll(r, half, 1) * s_right)


def _mla_lat_epilogue(acc, j, extra, outs, *, q_rank, kv_rank):
    gq_ref, gkv_ref, c_ref, sl_ref, sr_ref = extra
    cq_ref, ckv_ref, kr_ref = outs

    def rms(v, g):
        return v * lax.rsqrt(jnp.mean(v * v, axis=-1, keepdims=True) + RMS_EPS) * g

    cq_ref[...] = rms(acc[:, :q_rank], gq_ref[...]).astype(BF16)
    ckv_ref[...] = rms(acc[:, q_rank:q_rank + kv_rank], gkv_ref[...]).astype(BF16)
    kr = acc[:, q_rank + kv_rank:]
    kr_ref[...] = _rope64_in_chunk(kr, c_ref[...], sl_ref[...], sr_ref[...]).astype(BF16)


def _mla_q_epilogue(acc, j, extra, outs, *, heads_per_tile, scale):
    c_ref, sl_ref, sr_ref = extra
    (q_ref,) = outs
    c, sl, sr = c_ref[...], sl_ref[...], sr_ref[...]
    for h in range(heads_per_tile):
        lo = h * 2 * LANES
        q_ref[:, lo:lo + LANES] = (acc[:, lo:lo + LANES] * scale).astype(BF16)
        q_ref[:, lo + LANES:lo + 2 * LANES] = _rope64_in_chunk(
            acc[:, lo + LANES:lo + 2 * LANES], c, sl, sr).astype(BF16)


def _mla_kv_kernel(ckv_ref, wkt_ref, wv_ref, kr_ref, kt_ref, v_ref, *, heads_per_tile):
    x = ckv_ref[...]
    kn_t = _dot_nt(wkt_ref[...], x)
    vv = jnp.dot(x, wv_ref[...], preferred_element_type=F32)
    kr_t = kr_ref[...].astype(F32).T.astype(BF16)
    ones = jnp.ones((x.shape[0], LANES), BF16)
    for h in range(heads_per_tile):
        lo = h * 2 * LANES
        kt_ref[lo:lo + LANES, :] = kn_t[h * LANES:(h + 1) * LANES, :].astype(BF16)
        kt_ref[lo + LANES:lo + 2 * LANES, :] = kr_t
        v_ref[:, lo:lo + LANES] = vv[:, h * LANES:(h + 1) * LANES].astype(BF16)
        v_ref[:, lo + LANES:lo + 2 * LANES] = ones


def _causal_mask(tq, tk):
    row = lax.broadcasted_iota(jnp.int32, (tq, tk), 0)
    col = lax.broadcasted_iota(jnp.int32, (tq, tk), 1)
    return col <= row


def _dot_nt(a, b):
    return lax.dot_general(a, b, (((1,), (1,)), ((), ())), preferred_element_type=F32)


def _qk(q, k_t):
    return jnp.dot(q, k_t, preferred_element_type=F32)


def _softmax_step(s, v, m_ref, l_ref, acc_ref):
    m_prev = m_ref[...]
    m_new = jnp.maximum(m_prev, jnp.max(s, axis=-1, keepdims=True))
    p = jnp.exp2(s - jnp.tile(m_new, (1, s.shape[1] // LANES)))
    alpha = jnp.exp2(m_prev - m_new)
    if l_ref is not None:
        l_ref[...] = alpha * l_ref[...] + jnp.sum(p, axis=-1, keepdims=True)
    acc_ref[...] = (acc_ref[...] * jnp.tile(alpha, (1, acc_ref.shape[1] // LANES))
                    + jnp.dot(p.astype(BF16), v, preferred_element_type=F32))
    m_ref[...] = m_new


def _mla_attn_kernel(q_ref, k_ref, v_ref, o_ref, m_ref, acc_ref, *, blk, seq, heads):
    w = 2 * LANES

    def q_body(qi, carry):
        q0 = pl.multiple_of(qi * blk, blk)
        qs = [q_ref[pl.ds(q0, blk), h * w:(h + 1) * w] for h in range(heads)]
        m_ref[...] = jnp.full(m_ref.shape, NEG_INF, F32)
        acc_ref[...] = jnp.zeros(acc_ref.shape, F32)

        def step(k0, masked):
            scores = [_qk(qs[h], k_ref[h * w:(h + 1) * w, pl.ds(k0, blk)]) for h in range(heads)]
            for h in range(heads):
                s = scores[h]
                if masked:
                    s = jnp.where(_causal_mask(blk, blk), s, NEG_INF)
                _softmax_step(s, v_ref[pl.ds(k0, blk), h * w:(h + 1) * w], m_ref.at[h], None,
                              acc_ref.at[h])

        def kv_body(kj, c):
            step(pl.multiple_of(kj * blk, blk), False)
            return c

        lax.fori_loop(0, qi, kv_body, 0)
        step(q0, True)
        for h in range(heads):
            acc = acc_ref[h]
            o_ref[pl.ds(q0, blk), h * HEAD_DIM:(h + 1) * HEAD_DIM] = (
                acc[:, :HEAD_DIM] / acc[:, HEAD_DIM:]).astype(o_ref.dtype)
        return carry

    lax.fori_loop(0, seq // blk, q_body, 0)


def _diff_attn_kernel(q_ref, k_ref, v_ref, lam_ref, g_ref, o_ref,
                      m1_ref, l1_ref, a1_ref, m2_ref, l2_ref, a2_ref, *, blk, seq, lam_init):
    hd = HEAD_DIM
    lf = lam_ref[...]
    lam_full = (jnp.exp(jnp.sum(lf[0:1, :] * lf[1:2, :], axis=-1, keepdims=True))
                - jnp.exp(jnp.sum(lf[2:3, :] * lf[3:4, :], axis=-1, keepdims=True)) + lam_init)

    def q_body(qi, carry):
        q0 = pl.multiple_of(qi * blk, blk)
        q = q_ref[pl.ds(q0, blk), :]
        q1, q2 = q[:, :hd], q[:, hd:]
        for m_ref, l_ref, a_ref in ((m1_ref, l1_ref, a1_ref), (m2_ref, l2_ref, a2_ref)):
            m_ref[...] = jnp.full(m_ref.shape, NEG_INF, F32)
            l_ref[...] = jnp.zeros(l_ref.shape, F32)
            a_ref[...] = jnp.zeros(a_ref.shape, F32)

        def step(k0, masked):
            v = v_ref[pl.ds(k0, blk), :]
            s1 = _qk(q1, k_ref[:hd, pl.ds(k0, blk)])
            s2 = _qk(q2, k_ref[hd:, pl.ds(k0, blk)])
            if masked:
                mask = _causal_mask(blk, blk)
                s1 = jnp.where(mask, s1, NEG_INF)
                s2 = jnp.where(mask, s2, NEG_INF)
            _softmax_step(s1, v, m1_ref, l1_ref, a1_ref)
            _softmax_step(s2, v, m2_ref, l2_ref, a2_ref)

        def kv_body(kj, c):
            step(pl.multiple_of(kj * blk, blk), False)
            return c

        lax.fori_loop(0, qi, kv_body, 0)
        step(q0, True)
        reps = (1, 2 * hd // LANES)
        o = (a1_ref[...] / jnp.tile(l1_ref[...], reps)
             - lam_full * (a2_ref[...] / jnp.tile(l2_ref[...], reps)))
        o = o * lax.rsqrt(jnp.mean(o * o, axis=-1, keepdims=True) + LN_EPS) * g_ref[...]
        o_ref[pl.ds(q0, blk), :] = (o * (1.0 - lam_init)).astype(o_ref.dtype)
        return carry

    lax.fori_loop(0, seq // blk, q_body, 0)


def _sb_attn_kernel(q_ref, k_ref, v_ref, o_ref, run_ref, acc_ref, *, tq, tk, seq, heads):
    chunks_per_q = tq // tk
    hd = HEAD_DIM
    jr = lax.broadcasted_iota(jnp.int32, (2 * tk, tk), 0)
    jc = lax.broadcasted_iota(jnp.int32, (2 * tk, tk), 1)
    later_mat = (jnp.where(jr >= tk, jr - tk, jr) > jc).astype(BF16)

    def q_body(qi, carry):
        q0 = pl.multiple_of(qi * tq, tq)
        qs = [q_ref[pl.ds(q0, tq), h * hd:(h + 1) * hd] for h in range(heads)]
        run_ref[...] = jnp.zeros(run_ref.shape, F32)
        acc_ref[...] = jnp.zeros(acc_ref.shape, F32)

        def step(k0, diag_offset):
            logits = [_qk(qs[h], k_ref[h * hd:(h + 1) * hd, pl.ds(k0, tk)]) for h in range(heads)]
            for h in range(heads):
                z = logits[h]
                log_beta = jnp.minimum(z, 0.0) - jnp.log2(1.0 + jnp.exp2(-jnp.abs(z)))
                log_keep = log_beta - z
                if diag_offset is not None:
                    row = lax.broadcasted_iota(jnp.int32, (tq, tk), 0)
                    col = lax.broadcasted_iota(jnp.int32, (tq, tk), 1) + diag_offset
                    strict = col < row
                    log_keep = jnp.where(strict, log_keep, 0.0)
                hi = log_keep.astype(BF16)
                lo = (log_keep - hi.astype(F32)).astype(BF16)
                run = run_ref[h]
                later = (jnp.dot(jnp.concatenate([hi, lo], axis=1), later_mat,
                                 preferred_element_type=F32) + jnp.tile(run, (1, tk // LANES)))
                a = jnp.exp2(log_beta + later)
                if diag_offset is not None:
                    a = jnp.where(strict, a, 0.0)
                acc_ref[h] += jnp.dot(a.astype(BF16), v_ref[pl.ds(k0, tk), h * hd:(h + 1) * hd],
                                      preferred_element_type=F32)
                run_ref[h] = run + jnp.sum(log_keep, axis=-1, keepdims=True)

        for c in reversed(range(chunks_per_q)):
            step(q0 + c * tk, c * tk)

        def any_live():
            run_max = run_ref[0]
            for h in range(1, heads):
                run_max = jnp.maximum(run_max, run_ref[h])
            return jnp.max(run_max) > SB_DEAD_LOG2

        def kv_cond(carry):
            n, live = carry
            return jnp.logical_and(n < qi * chunks_per_q, live)

        def kv_body(carry):
            n, _ = carry
            kj = qi * chunks_per_q - 1 - n
            step(pl.multiple_of(kj * tk, tk), None)
            return n + 1, any_live()

        lax.while_loop(kv_cond, kv_body, (jnp.int32(0), any_live()))
        for h in range(heads):
            o_ref[pl.ds(q0, tq), h * hd:(h + 1) * hd] = acc_ref[h].astype(o_ref.dtype)
        return carry

    lax.fori_loop(0, seq // tq, q_body, 0)


def _attention(kernel, q, k, v, *, batch, seq, groups, k_off, v_off, dq, dv, d_out, scratch,
               vmem_mib, name):
    in_specs = [
        pl.BlockSpec((seq, dq), lambda b, g: (b, g)),
        pl.BlockSpec((dq, seq), lambda b, g: (k_off + g, b)),
        pl.BlockSpec((seq, dv), lambda b, g: (b, v_off + g)),
    ]
    return pl.pallas_call(
        kernel,
        grid=(batch, groups),
        in_specs=in_specs,
        out_specs=pl.BlockSpec((seq, d_out), lambda b, g: (b, g)),
        out_shape=jax.ShapeDtypeStruct((batch * seq, groups * d_out), BF16),
        scratch_shapes=scratch,
        compiler_params=_cparams(("parallel", "parallel"), vmem_mib),
        name=name,
    )(q, k, v)


def _mla_mixer(x2d, mod, seq, batch, w_in, q_norm_g, kv_norm_g, w_uq, w_ukv):
    t, d = x2d.shape
    q_rank, kv_rank = q_norm_g.shape[0], kv_norm_g.shape[0]
    heads = w_uq.shape[1] // (HEAD_DIM + MLA_ROPE_DIM)
    scale = (HEAD_DIM + MLA_ROPE_DIM) ** -0.5 * LOG2_E
    tm = min(512, seq)
    row_blocks = seq // tm

    lat_n = q_rank + kv_rank + LANES
    w_in_p = jnp.zeros((d, lat_n), BF16).at[:, :w_in.shape[1]].set(w_in.astype(BF16))
    c_k, sl_k, sr_k = _mla_rope_tables(seq, 1.0)
    c_q, sl_q, sr_q = _mla_rope_tables(seq, scale)
    tab_spec = ((tm, LANES), lambda i, j: (i % row_blocks, 0))

    cq, ckv, kr = _proj(
        x2d, w_in_p, tm=tm, tn=lat_n, mod=mod, seq=seq,
        extras=[(q_norm_g.reshape(1, q_rank), (1, q_rank), lambda i, j: (0, 0)),
                (kv_norm_g.reshape(1, kv_rank), (1, kv_rank), lambda i, j: (0, 0)),
                (c_k,) + tab_spec, (sl_k,) + tab_spec, (sr_k,) + tab_spec],
        outs=[((t, q_rank), BF16, (tm, q_rank), lambda i, j: (i, 0)),
              ((t, kv_rank), BF16, (tm, kv_rank), lambda i, j: (i, 0)),
              ((t, LANES), BF16, (tm, LANES), lambda i, j: (i, 0))],
        epilogue=functools.partial(_mla_lat_epilogue, q_rank=q_rank, kv_rank=kv_rank),
        name="mla_latent")

    w_q = w_uq.reshape(q_rank, heads, HEAD_DIM + MLA_ROPE_DIM).astype(BF16)
    w_q = jnp.pad(w_q, ((0, 0), (0, 0), (0, 2 * LANES - HEAD_DIM - MLA_ROPE_DIM)))
    w_q = w_q.reshape(q_rank, heads * 2 * LANES)
    hpt = min(heads, 4)
    (q_cat,) = _proj(
        cq, w_q, tm=tm, tn=hpt * 2 * LANES, mod=None, seq=seq,
        extras=[(c_q,) + tab_spec, (sl_q,) + tab_spec, (sr_q,) + tab_spec],
        outs=[((t, heads * 2 * LANES), BF16, (tm, hpt * 2 * LANES), lambda i, j: (i, j))],
        epilogue=functools.partial(_mla_q_epilogue, heads_per_tile=hpt, scale=scale),
        name="mla_q_up")

    w_kv = w_ukv.reshape(kv_rank, heads, 2 * HEAD_DIM).astype(BF16)
    w_kt = w_kv[:, :, :HEAD_DIM].reshape(kv_rank, heads * HEAD_DIM).T
    w_v = w_kv[:, :, HEAD_DIM:].reshape(kv_rank, heads * HEAD_DIM)
    k_cat, v = pl.pallas_call(
        functools.partial(_mla_kv_kernel, heads_per_tile=hpt),
        grid=(t // tm, heads // hpt),
        in_specs=[
            pl.BlockSpec((tm, kv_rank), lambda i, j: (i, 0)),
            pl.BlockSpec((hpt * HEAD_DIM, kv_rank), lambda i, j: (j, 0)),
            pl.BlockSpec((kv_rank, hpt * HEAD_DIM), lambda i, j: (0, j)),
            pl.BlockSpec((tm, LANES), lambda i, j: (i, 0)),
        ],
        out_specs=[
            pl.BlockSpec((hpt * 2 * LANES, tm), lambda i, j: (j, i)),
            pl.BlockSpec((tm, hpt * 2 * LANES), lambda i, j: (i, j)),
        ],
        out_shape=[jax.ShapeDtypeStruct((heads * 2 * LANES, t), BF16),
                   jax.ShapeDtypeStruct((t, heads * 2 * LANES), BF16)],
        compiler_params=_cparams(("parallel", "parallel"), 32),
        name="mla_kv_up",
    )(ckv, w_kt, w_v, kr)

    blk = min(512, seq)
    hg = ATTN_HEAD_GROUP
    return _attention(
        functools.partial(_mla_attn_kernel, blk=blk, seq=seq, heads=hg), q_cat, k_cat, v,
        batch=batch, seq=seq, groups=heads // hg, k_off=0, v_off=0,
        dq=hg * 2 * LANES, dv=hg * 2 * LANES, d_out=hg * HEAD_DIM,
        scratch=[pltpu.VMEM((hg, blk, LANES), F32), pltpu.VMEM((hg, blk, 2 * LANES), F32)],
        vmem_mib=48, name="mla_attention")


def _qkv_out_specs(t, width, tm, tn):
    nq = width // tn

    def qv_map(i, j):
        return i, jnp.where(j < nq, j, jnp.where(j < 2 * nq, nq - 1, j - nq))

    def kt_map(i, j):
        return jnp.clip(j - nq, 0, nq - 1), i

    return [((t, 2 * width), BF16, (tm, tn), qv_map), ((width, t), BF16, (tn, tm), kt_map)]


def _diff_qkv_epilogue(acc, j, extra, outs, *, tn, width, scale):
    cos_ref, sin_ref = extra
    qv_ref, kt_ref = outs

    def rope(factor):
        cos = cos_ref[...] * factor
        sin = sin_ref[...] * factor
        chunks = []
        for c in range(tn // LANES):
            a = acc[:, c * LANES:(c + 1) * LANES]
            chunks.append(a * cos + pltpu.roll(a, HEAD_DIM // 2, 1) * sin)
        return jnp.concatenate(chunks, axis=1)

    @pl.when(j * tn < width)
    def _():
        qv_ref[...] = rope(scale).astype(BF16)

    @pl.when(jnp.logical_and(j * tn >= width, j * tn < 2 * width))
    def _():
        kt_ref[...] = rope(1.0).T.astype(BF16)

    @pl.when(j * tn >= 2 * width)
    def _():
        qv_ref[...] = acc.astype(BF16)


def _diff_mixer(x2d, mod, seq, batch, layer_idx, w_qkv, lam, subln_g):
    t, d = x2d.shape
    width = w_qkv.shape[1] // 3
    heads = width // (2 * HEAD_DIM)
    scale = HEAD_DIM ** -0.5 * LOG2_E
    lam_init = 0.8 - 0.6 * math.exp(-0.3 * layer_idx)
    tm = min(512, seq)
    tn = min(1024, width)
    row_blocks = seq // tm
    cos, sin = _rope_tables(seq, HEAD_DIM)
    cos_t = jnp.concatenate([cos, cos], axis=1)
    sin_t = jnp.concatenate([-sin, sin], axis=1)
    tab_spec = ((tm, LANES), lambda i, j: (i % row_blocks, 0))
    qv, k_t = _proj(
        x2d, w_qkv.astype(BF16), tm=tm, tn=tn, mod=mod, seq=seq,
        extras=[(cos_t,) + tab_spec, (sin_t,) + tab_spec],
        outs=_qkv_out_specs(t, width, tm, tn),
        epilogue=functools.partial(_diff_qkv_epilogue, tn=tn, width=width, scale=scale),
        name="diff_qkv")
    blk = min(512, seq)
    dh = 2 * HEAD_DIM
    in_specs = [
        pl.BlockSpec((seq, dh), lambda b, h: (b, h)),
        pl.BlockSpec((dh, seq), lambda b, h: (h, b)),
        pl.BlockSpec((seq, dh), lambda b, h: (b, heads + h)),
        pl.BlockSpec((4, HEAD_DIM), lambda b, h: (0, 0)),
        pl.BlockSpec((1, dh), lambda b, h: (0, 0)),
    ]
    stat = lambda: pltpu.VMEM((blk, LANES), F32)
    accum = lambda: pltpu.VMEM((blk, dh), F32)
    return pl.pallas_call(
        functools.partial(_diff_attn_kernel, blk=blk, seq=seq, lam_init=lam_init),
        grid=(batch, heads),
        in_specs=in_specs,
        out_specs=pl.BlockSpec((seq, dh), lambda b, h: (b, h)),
        out_shape=jax.ShapeDtypeStruct((t, width), BF16),
        scratch_shapes=[stat(), stat(), accum(), stat(), stat(), accum()],
        compiler_params=_cparams(("parallel", "parallel"), 40),
        name="diff_attention",
    )(qv, k_t, qv, lam, subln_g.reshape(1, dh))


def _sb_qkv_epilogue(acc, j, extra, outs, *, tn, width, scale):
    qv_ref, kt_ref = outs

    @pl.when(j * tn < width)
    def _():
        qv_ref[...] = (acc * scale).astype(BF16)

    @pl.when(jnp.logical_and(j * tn >= width, j * tn < 2 * width))
    def _():
        kt_ref[...] = acc.T.astype(BF16)

    @pl.when(j * tn >= 2 * width)
    def _():
        qv_ref[...] = acc.astype(BF16)


def _sb_mixer(x2d, mod, seq, batch, w_qkv):
    t, d = x2d.shape
    width = w_qkv.shape[1] // 3
    heads = width // HEAD_DIM
    scale = HEAD_DIM ** -0.5 * LOG2_E
    tm = min(512, seq)
    tn = min(1024, width)
    qv, k_t = _proj(
        x2d, w_qkv.astype(BF16), tm=tm, tn=tn, mod=mod, seq=seq, extras=[],
        outs=_qkv_out_specs(t, width, tm, tn),
        epilogue=functools.partial(_sb_qkv_epilogue, tn=tn, width=width, scale=scale),
        name="sb_qkv")
    tq = min(512, seq)
    tk = min(256, seq)
    hg = ATTN_HEAD_GROUP
    groups = heads // hg
    return _attention(
        functools.partial(_sb_attn_kernel, tq=tq, tk=tk, seq=seq, heads=hg), qv, k_t, qv,
        batch=batch, seq=seq, groups=groups, k_off=0, v_off=groups,
        dq=hg * HEAD_DIM, dv=hg * HEAD_DIM, d_out=hg * HEAD_DIM,
        scratch=[pltpu.VMEM((hg, tq, LANES), F32), pltpu.VMEM((hg, tq, HEAD_DIM), F32)],
        vmem_mib=40, name="sb_attention")


def _split_bf16(v):
    hi = v.astype(BF16)
    return hi, (v - hi.astype(F32)).astype(BF16)


def _outproj_router_kernel(o_ref, w_ref, x_ref, mod_ref, g_ref, b_ref, rw_ref, rb_ref,
                           x1_ref, idx_ref, gate_ref, *, alpha, n_experts):
    y = jnp.dot(o_ref[...], w_ref[...], preferred_element_type=F32)
    mod = mod_ref[...]
    x1 = _layer_norm_rows(alpha * x_ref[...] + mod[2:3, :] * y, g_ref[...], b_ref[...])
    x1_ref[...] = x1
    h2 = x1 * (1.0 + mod[4:5, :]) + mod[3:4, :]
    hi, lo = _split_bf16(h2)
    logits = jnp.dot(jnp.concatenate([hi, lo, hi], axis=1), rw_ref[...],
                     preferred_element_type=F32) + rb_ref[...]
    lane = lax.broadcasted_iota(jnp.int32, logits.shape, 1).astype(F32)
    work = jnp.where(lane < n_experts, logits, -jnp.inf)
    idx_out = jnp.zeros(logits.shape, F32)
    val_out = jnp.zeros(logits.shape, F32)
    top = None
    denom = None
    for kk in range(TOP_K):
        m = jnp.max(work, axis=-1, keepdims=True)
        first = jnp.min(jnp.where(work == m, lane, float(LANES)), axis=-1, keepdims=True)
        if kk == 0:
            top = m
        e = jnp.exp(m - top)
        denom = e if kk == 0 else denom + e
        idx_out = jnp.where(lane == kk, first, idx_out)
        val_out = jnp.where(lane == kk, e, val_out)
        work = jnp.where(lane == first, -jnp.inf, work)
    idx_ref[...] = idx_out.astype(jnp.int32)
    gate_ref[...] = val_out / denom


def _outproj_router(o, w_o, x2d, mod, ln_g, ln_b, router_w, router_b, *, seq, alpha):
    t, d = x2d.shape
    n_experts = router_w.shape[1]
    tm = min(256, seq)
    blocks_per_seq = seq // tm
    once = pl.Buffered(1)
    rw = jnp.zeros((d, LANES), F32).at[:, :n_experts].set(router_w)
    rw_hi, rw_lo = _split_bf16(rw)
    rw3 = jnp.concatenate([rw_hi, rw_hi, rw_lo], axis=0)
    rb = jnp.zeros((1, LANES), F32).at[0, :n_experts].set(router_b)
    row = lambda i: (i, 0)
    fixed = lambda i: (0, 0)
    return pl.pallas_call(
        functools.partial(_outproj_router_kernel, alpha=alpha, n_experts=n_experts),
        grid=(t // tm,),
        in_specs=[
            pl.BlockSpec((tm, o.shape[1]), row),
            pl.BlockSpec(w_o.shape, fixed, pipeline_mode=once),
            pl.BlockSpec((tm, d), row),
            pl.BlockSpec((None, 6, d), lambda i: (i // blocks_per_seq, 0, 0)),
            pl.BlockSpec((1, d), fixed),
            pl.BlockSpec((1, d), fixed),
            pl.BlockSpec((3 * d, LANES), fixed, pipeline_mode=once),
            pl.BlockSpec((1, LANES), fixed),
        ],
        out_specs=[pl.BlockSpec((tm, d), row),
                   pl.BlockSpec((tm, LANES), row), pl.BlockSpec((tm, LANES), row)],
        out_shape=[jax.ShapeDtypeStruct((t, d), F32),
                   jax.ShapeDtypeStruct((t, LANES), jnp.int32), jax.ShapeDtypeStruct((t, LANES), F32)],
        compiler_params=_cparams(("parallel",), 48),
        name="outproj_ln_router",
    )(o, w_o.astype(BF16), x2d, mod, ln_g.reshape(1, d), ln_b.reshape(1, d), rw3, rb)


def _plan_kernel(idx_ref, rank_ref, cnt_ref, run_ref):
    @pl.when(pl.program_id(0) == 0)
    def _():
        run_ref[...] = jnp.zeros(run_ref.shape, F32)

    idx = idx_ref[...]
    tb = idx.shape[0]
    lane = lax.broadcasted_iota(jnp.int32, idx.shape, 1)
    onehots = [(lane == idx[:, kk:kk + 1]).astype(F32) for kk in range(TOP_K)]
    chosen = onehots[0]
    for kk in range(1, TOP_K):
        chosen = chosen + onehots[kk]
    earlier = (lax.broadcasted_iota(jnp.int32, (tb, tb), 1)
               < lax.broadcasted_iota(jnp.int32, (tb, tb), 0)).astype(BF16)
    prefix = jnp.dot(earlier, chosen.astype(BF16), preferred_element_type=F32) + run_ref[0:1, :]
    rank = jnp.zeros(idx.shape, F32)
    for kk in range(TOP_K):
        rank = jnp.where(lane == kk, jnp.sum(prefix * onehots[kk], axis=-1, keepdims=True), rank)
    rank_ref[...] = rank.astype(jnp.int32)
    run_ref[...] = run_ref[...] + jnp.sum(chosen, axis=0, keepdims=True)
    cnt_ref[...] = run_ref[...]


def _routing_plan(idx_pad, n_experts, tm):
    t = idx_pad.shape[0]
    tb = min(512, t)
    rank_pad, cnt = pl.pallas_call(
        _plan_kernel,
        grid=(t // tb,),
        in_specs=[pl.BlockSpec((tb, LANES), lambda i: (i, 0))],
        out_specs=[pl.BlockSpec((tb, LANES), lambda i: (i, 0)),
                   pl.BlockSpec((SUBLANES, LANES), lambda i: (0, 0))],
        out_shape=[jax.ShapeDtypeStruct((t, LANES), jnp.int32),
                   jax.ShapeDtypeStruct((SUBLANES, LANES), F32)],
        scratch_shapes=[pltpu.VMEM((SUBLANES, LANES), F32)],
        compiler_params=_cparams(("arbitrary",), 16),
        name="moe_plan",
    )(idx_pad)
    counts = cnt[0, :n_experts].astype(jnp.int32)
    padded = (counts + tm - 1) // tm * tm
    pad_end = jnp.cumsum(padded)
    pad_start = pad_end - padded
    chose = idx_pad[:, :TOP_K, None] == jnp.arange(n_experts, dtype=jnp.int32)
    dest = (jnp.sum(jnp.where(chose, pad_start, 0), axis=-1) + rank_pad[:, :TOP_K]).astype(jnp.int32)
    n_blocks = t * TOP_K // tm + n_experts
    block_start = jnp.arange(n_blocks, dtype=jnp.int32) * tm
    block_expert = jnp.minimum(jnp.sum(block_start[:, None] >= pad_end[None, :], axis=1),
                               n_experts - 1).astype(jnp.int32)
    n_used = (pad_end[-1] // tm).astype(jnp.int32).reshape(1)
    fill_start = jnp.concatenate([pad_start + counts, pad_end[-1:]]).astype(jnp.int32)
    fill_count = jnp.concatenate([pad_end - pad_start - counts,
                                  n_blocks * tm - pad_end[-1:]]).astype(jnp.int32)
    prev_expert = jnp.concatenate([jnp.full((1,), -1, jnp.int32), block_expert[:-1]])
    run_first = (block_expert != prev_expert).astype(jnp.int32)
    ids = jnp.arange(n_experts, dtype=jnp.int32)
    later_nonempty = jnp.logical_and(ids[None, :] > ids[:, None], counts[None, :] > 0)
    next_nonempty = jnp.min(jnp.where(later_nonempty, ids[None, :], n_experts), axis=1)
    run_next = jnp.where(next_nonempty < n_experts, next_nonempty, -1).astype(jnp.int32)[block_expert]
    return dict(dest=dest.reshape(-1), block_expert=block_expert, n_used=n_used, fill_start=fill_start,
                fill_count=fill_count, n_blocks=n_blocks, run_first=run_first, run_next=run_next)


def _row_pitch(chunks):
    return chunks + 1


def _load_rows(ref, lead, n_rows, chunks):
    pitch = _row_pitch(chunks)
    return jnp.concatenate(
        [ref[lead + (pl.ds(c, n_rows, stride=pitch), slice(None))] for c in range(chunks)], axis=1)


def _store_rows(ref, lead, val, chunks):
    pitch = _row_pitch(chunks)
    n_rows = val.shape[0]
    for c in range(chunks):
        ref[lead + (pl.ds(c, n_rows, stride=pitch), slice(None))] = val[:, c * LANES:(c + 1) * LANES]
    ref[lead + (pl.ds(chunks, n_rows, stride=pitch), slice(None))] = jnp.zeros((n_rows, LANES), val.dtype)


def _dispatch_kernel(dest_ref, fstart_ref, fcount_ref, x_ref, mod_ref, xs_hbm, hbuf, zrow, sem, fsem,
                     *, tb, n_steps, n_fills, chunks):
    i = pl.program_id(0)
    slot = i % 2
    pitch = _row_pitch(chunks)

    def drain(buf_slot):
        for _ in range(TOP_K):
            pltpu.make_async_copy(hbuf.at[buf_slot], xs_hbm.at[pl.ds(0, tb * pitch)],
                                  sem.at[buf_slot]).wait()

    def for_each_fill(fn):
        def per_range(f, c):
            start = fstart_ref[f]

            def per_row(j, c2):
                fn(pltpu.make_async_copy(zrow, xs_hbm.at[pl.ds((start + j) * pitch, pitch)], fsem))
                return c2

            lax.fori_loop(0, fcount_ref[f], per_row, 0)
            return c

        lax.fori_loop(0, n_fills, per_range, 0)

    @pl.when(i == 0)
    def _():
        zrow[...] = jnp.zeros(zrow.shape, F32)
        for_each_fill(lambda copy: copy.start())

    @pl.when(i >= 2)
    def _():
        drain(slot)

    mod = mod_ref[...]
    h2 = x_ref[...] * (1.0 + mod[4:5, :]) + mod[3:4, :]
    _store_rows(hbuf, (slot,), h2, chunks)

    def per_token(r, c):
        base = (i * tb + r) * TOP_K
        src = hbuf.at[slot, pl.ds(r * pitch, pitch)]
        for kk in range(TOP_K):
            pltpu.make_async_copy(src, xs_hbm.at[pl.ds(dest_ref[base + kk] * pitch, pitch)],
                                  sem.at[slot]).start()
        return c

    lax.fori_loop(0, tb, per_token, 0, unroll=4)

    @pl.when(i == n_steps - 1)
    def _():
        drain(slot)
        if n_steps > 1:
            drain(1 - slot)
        for_each_fill(lambda copy: copy.wait())


def _dispatch(x1, mod, plan, *, seq, tm):
    dest, fill_start, fill_count = plan["dest"], plan["fill_start"], plan["fill_count"]
    n_blocks = plan["n_blocks"]
    t, d = x1.shape
    chunks = d // LANES
    pitch = _row_pitch(chunks)
    tb = min(256, seq)
    n_steps = t // tb
    blocks_per_seq = seq // tb
    grid_spec = pltpu.PrefetchScalarGridSpec(
        num_scalar_prefetch=3,
        grid=(n_steps,),
        in_specs=[
            pl.BlockSpec((tb, d), lambda i, *_: (i, 0)),
            pl.BlockSpec((None, 6, d), lambda i, *_: (i // blocks_per_seq, 0, 0)),
        ],
        out_specs=pl.BlockSpec(memory_space=pl.ANY),
        scratch_shapes=[pltpu.VMEM((2, tb * pitch, LANES), F32), pltpu.VMEM((pitch, LANES), F32),
                        pltpu.SemaphoreType.DMA((2,)), pltpu.SemaphoreType.DMA(())],
    )
    return pl.pallas_call(
        functools.partial(_dispatch_kernel, tb=tb, n_steps=n_steps, n_fills=fill_start.shape[0],
                          chunks=chunks),
        grid_spec=grid_spec,
        out_shape=jax.ShapeDtypeStruct((n_blocks * tm * pitch, LANES), F32),
        compiler_params=_cparams(("arbitrary",), 32),
        name="moe_dispatch",
    )(dest, fill_start, fill_count, x1, mod)


def _expert_kernel(be_ref, used_ref, first_ref, next_ref, xs_ref, wgu_hbm, bgu_ref, wd_hbm, bd_ref,
                   y_ref, gu_stage, dn_stage, gu_bf16, dn_bf16, sem, *, layer, tm, d_expert, chunks):
    i = pl.program_id(0)

    def weight_copies(expert):
        return (pltpu.make_async_copy(wgu_hbm.at[layer, expert], gu_stage, sem.at[0]),
                pltpu.make_async_copy(wd_hbm.at[layer, expert], dn_stage, sem.at[1]))

    def cast(stage, out):
        n_rows = stage.shape[0]
        slab = math.gcd(n_rows, 256)

        def body(j, c):
            rows = pl.ds(pl.multiple_of(j * slab, slab), slab)
            out[rows, :] = stage[rows, :].astype(BF16)
            return c
        lax.fori_loop(0, n_rows // slab, body, 0)

    @pl.when(i == 0)
    def _():
        for copy in weight_copies(be_ref[0]):
            copy.start()

    @pl.when(jnp.logical_and(i < used_ref[0], first_ref[i] == 1))
    def _():
        for copy in weight_copies(be_ref[i]):
            copy.wait()
        cast(gu_stage, gu_bf16)
        cast(dn_stage, dn_bf16)

        @pl.when(next_ref[i] >= 0)
        def _():
            for copy in weight_copies(next_ref[i]):
                copy.start()

    @pl.when(i < used_ref[0])
    def _():
        x = _load_rows(xs_ref, (), tm, chunks).astype(BF16)
        gu = jnp.dot(x, gu_bf16[...], preferred_element_type=F32) + bgu_ref[...]
        g = jnp.minimum(gu[:, :d_expert], SWIGLU_LIMIT)
        lin = jnp.clip(gu[:, d_expert:], -SWIGLU_LIMIT, SWIGLU_LIMIT)
        act = g * jax.nn.sigmoid(SWIGLU_ALPHA * g) * (lin + 1.0)
        y = jnp.dot(act.astype(BF16), dn_bf16[...], preferred_element_type=F32) + bd_ref[...]
        _store_rows(y_ref, (), y, chunks)

    @pl.when(i >= used_ref[0])
    def _():
        y_ref[...] = jnp.zeros(y_ref.shape, F32)


def _experts(xs, plan, layer, w_gate_up, b_gate_up, w_down, b_down, tm):
    n_blocks = plan["n_blocks"]
    depth, n_experts, d_expert, d = w_down.shape
    chunks = d // LANES
    pitch = _row_pitch(chunks)
    per_expert = lambda i, be, used, first, nxt: (layer, be[i], 0, 0)
    rows = lambda i, be, used, first, nxt: (i, 0)
    used_rows = lambda i, be, used, first, nxt: (jnp.minimum(i, used[0] - 1), 0)
    grid_spec = pltpu.PrefetchScalarGridSpec(
        num_scalar_prefetch=4,
        grid=(n_blocks,),
        in_specs=[
            pl.BlockSpec((tm * pitch, LANES), used_rows),
            pl.BlockSpec(memory_space=pl.ANY),
            pl.BlockSpec((None, None, 1, 2 * d_expert), per_expert),
            pl.BlockSpec(memory_space=pl.ANY),
            pl.BlockSpec((None, None, 1, d), per_expert),
        ],
        out_specs=pl.BlockSpec((tm * pitch, LANES), rows),
        scratch_shapes=[pltpu.VMEM((d, 2 * d_expert), F32), pltpu.VMEM((d_expert, d), F32),
                        pltpu.VMEM((d, 2 * d_expert), BF16), pltpu.VMEM((d_expert, d), BF16),
                        pltpu.SemaphoreType.DMA((2,))],
    )
    return pl.pallas_call(
        functools.partial(_expert_kernel, layer=layer, tm=tm, d_expert=d_expert, chunks=chunks),
        grid_spec=grid_spec,
        out_shape=jax.ShapeDtypeStruct((n_blocks * tm * pitch, LANES), F32),
        compiler_params=_cparams(("arbitrary",), 56),
        name="moe_experts",
    )(plan["block_expert"], plan["n_used"], plan["run_first"], plan["run_next"], xs, w_gate_up,
      b_gate_up.reshape(depth, n_experts, 1, 2 * d_expert), w_down,
      b_down.reshape(depth, n_experts, 1, d))


def _combine_kernel(dest_ref, ys_hbm, x_ref, gate_ref, mod_ref, g_ref, b_ref, o_ref, ybuf, sem,
                    *, tn, n_blocks, alpha, chunks):
    i = pl.program_id(0)
    slot = i % 2
    pitch = _row_pitch(chunks)

    def issue(block, buf_slot):
        def body(r, c):
            base = (block * tn + r) * TOP_K
            for kk in range(TOP_K):
                pltpu.make_async_copy(ys_hbm.at[pl.ds(dest_ref[base + kk] * pitch, pitch)],
                                      ybuf.at[buf_slot, kk, pl.ds(r * pitch, pitch)],
                                      sem.at[buf_slot]).start()
            return c
        lax.fori_loop(0, tn, body, 0, unroll=4)

    @pl.when(i == 0)
    def _():
        issue(0, 0)

    @pl.when(i + 1 < n_blocks)
    def _():
        issue(i + 1, 1 - slot)

    for kk in range(TOP_K):
        pltpu.make_async_copy(ys_hbm.at[pl.ds(0, tn * pitch)], ybuf.at[slot, kk], sem.at[slot]).wait()

    gate = gate_ref[...]
    y = None
    for kk in range(TOP_K):
        term = gate[:, kk:kk + 1] * _load_rows(ybuf, (slot, kk), tn, chunks)
        y = term if y is None else y + term
    mod = mod_ref[...]
    o_ref[...] = _layer_norm_rows(alpha * x_ref[...] + mod[5:6, :] * y, g_ref[...], b_ref[...])


def _combine(ys, dest, x1, gate_pad, mod, ln_g, ln_b, *, seq, alpha):
    t, d = x1.shape
    chunks = d // LANES
    pitch = _row_pitch(chunks)
    tn = min(128, seq)
    n_blocks = t // tn
    blocks_per_seq = seq // tn
    grid_spec = pltpu.PrefetchScalarGridSpec(
        num_scalar_prefetch=1,
        grid=(n_blocks,),
        in_specs=[
            pl.BlockSpec(memory_space=pl.ANY),
            pl.BlockSpec((tn, d), lambda i, dst: (i, 0)),
            pl.BlockSpec((tn, LANES), lambda i, dst: (i, 0)),
            pl.BlockSpec((None, 6, d), lambda i, dst: (i // blocks_per_seq, 0, 0)),
            pl.BlockSpec((1, d), lambda i, dst: (0, 0)),
            pl.BlockSpec((1, d), lambda i, dst: (0, 0)),
        ],
        out_specs=pl.BlockSpec((tn, d), lambda i, dst: (i, 0)),
        scratch_shapes=[pltpu.VMEM((2, TOP_K, tn * pitch, LANES), F32), pltpu.SemaphoreType.DMA((2,))],
    )
    return pl.pallas_call(
        functools.partial(_combine_kernel, tn=tn, n_blocks=n_blocks, alpha=alpha, chunks=chunks),
        grid_spec=grid_spec,
        out_shape=jax.ShapeDtypeStruct((t, d), F32),
        compiler_params=_cparams(("arbitrary",), 40),
        name="moe_combine_ln",
    )(dest, ys, x1, gate_pad, mod, ln_g.reshape(1, d), ln_b.reshape(1, d))


def kernel(x, c, ada_w, ada_b, ln_g, ln_b, mla_w_in, mla_q_norm_g, mla_kv_norm_g, mla_w_uq, mla_w_ukv,
           mla_w_o, diff_w_qkv, diff_lambda, diff_subln_g, diff_w_o, sb_w_qkv, sb_w_o, moe_router_w,
           moe_router_b, moe_w_gate_up, moe_b_gate_up, moe_w_down, moe_b_down):
    batch, seq, d = x.shape
    depth = ada_w.shape[0]
    alpha = (2 * depth) ** 0.25
    n_experts = moe_router_w.shape[-1]
    moe_tm = 256
    mods = _adaln(c, ada_w, ada_b)
    x2d = x.reshape(batch * seq, d)
    for i in range(depth):
        mod = mods[i]
        kind, j = i % N_MIXERS, i // N_MIXERS
        if kind == 0:
            o = _mla_mixer(x2d, mod, seq, batch, mla_w_in[j], mla_q_norm_g[j], mla_kv_norm_g[j],
                           mla_w_uq[j], mla_w_ukv[j])
            w_o = mla_w_o[j]
        elif kind == 1:
            o = _diff_mixer(x2d, mod, seq, batch, i, diff_w_qkv[j], diff_lambda[j], diff_subln_g[j])
            w_o = diff_w_o[j]
        else:
            o = _sb_mixer(x2d, mod, seq, batch, sb_w_qkv[j])
            w_o = sb_w_o[j]
        x1, idx_pad, gate_pad = _outproj_router(
            o, w_o, x2d, mod, ln_g[i, 0], ln_b[i, 0], moe_router_w[i], moe_router_b[i],
            seq=seq, alpha=alpha)
        plan = _routing_plan(idx_pad, n_experts, moe_tm)
        xs = _dispatch(x1, mod, plan, seq=seq, tm=moe_tm)
        ys = _experts(xs, plan, i, moe_w_gate_up, moe_b_gate_up, moe_w_down, moe_b_down, moe_tm)
        x2d = _combine(ys, plan["dest"], x1, gate_pad, mod, ln_g[i, 1], ln_b[i, 1], seq=seq,
                       alpha=alpha)
    return x2d.reshape(batch, seq, d)
```

```python
import functools
import math

import jax
import jax.numpy as jnp
from jax import lax
from jax.experimental import pallas as pl
from jax.experimental.pallas import tpu as pltpu

F32 = jnp.float32
BF16 = jnp.bfloat16

N_MIXERS = 3
NEG_INF = -1e30
ROPE_THETA = 10000.0
LN_EPS = 1e-5
RMS_EPS = 1e-6
HEAD_DIM = 128
MLA_ROPE_DIM = 64
TOP_K = 4
SWIGLU_LIMIT = 7.0
SWIGLU_ALPHA = 1.702
LANES = 128
SUBLANES = 8
LOG2_E = 1.4426950408889634
ATTN_HEAD_GROUP = 2
SB_DEAD_LOG2 = -160.0
MIB = 1024 * 1024


def _cparams(sems, vmem_mib):
    return pltpu.CompilerParams(dimension_semantics=sems, vmem_limit_bytes=vmem_mib * MIB)


def _layer_norm_rows(r, g, b):
    mu = jnp.mean(r, axis=-1, keepdims=True)
    d = r - mu
    var = jnp.mean(d * d, axis=-1, keepdims=True)
    return d * lax.rsqrt(var + LN_EPS) * g + b


def _adaln_kernel(c_ref, w_ref, b_ref, o_ref):
    c = c_ref[...]
    cond = c * jax.nn.sigmoid(c)
    o_ref[...] = jnp.dot(cond.astype(BF16), w_ref[...].astype(BF16),
                         preferred_element_type=F32) + b_ref[...]


def _adaln(c, ada_w, ada_b):
    depth, d, n = ada_w.shape
    b = c.shape[0]
    rows = 16
    tn = next(cand for cand in (1024, 512, 256, 128) if n % cand == 0)
    c_pad = jnp.zeros((rows, d), F32).at[:b].set(c)
    out = pl.pallas_call(
        _adaln_kernel,
        grid=(depth, n // tn),
        in_specs=[
            pl.BlockSpec((rows, d), lambda l, j: (0, 0)),
            pl.BlockSpec((None, d, tn), lambda l, j: (l, 0, j)),
            pl.BlockSpec((None, 1, tn), lambda l, j: (l, 0, j)),
        ],
        out_specs=pl.BlockSpec((None, rows, tn), lambda l, j: (l, 0, j)),
        out_shape=jax.ShapeDtypeStruct((depth, rows, n), F32),
        compiler_params=_cparams(("parallel", "parallel"), 40),
        name="adaln",
    )(c_pad, ada_w, ada_b.reshape(depth, 1, n))
    return out[:, :b].reshape(depth, b, 6, d)


def _proj_kernel(*refs, has_mod, n_extra, n_out, epilogue):
    x_ref = refs[0]
    pos = 1
    mod_ref = None
    if has_mod:
        mod_ref = refs[pos]
        pos += 1
    w_ref = refs[pos]
    pos += 1
    extra = refs[pos:pos + n_extra]
    pos += n_extra
    outs = refs[pos:pos + n_out]
    pos += n_out
    if has_mod:
        h_ref = refs[pos]

        @pl.when(pl.program_id(1) == 0)
        def _():
            mod = mod_ref[...]
            h_ref[...] = (x_ref[...] * (1.0 + mod[1:2, :]) + mod[0:1, :]).astype(BF16)

        h = h_ref[...]
    else:
        h = x_ref[...]
    acc = jnp.dot(h, w_ref[...], preferred_element_type=F32)
    epilogue(acc, pl.program_id(1), extra, outs)


def _proj(x, w, *, tm, tn, mod, seq, extras, outs, epilogue, vmem_mib=48, name):
    t, k = x.shape
    n = w.shape[1]
    in_specs = [pl.BlockSpec((tm, k), lambda i, j: (i, 0))]
    args = [x]
    if mod is not None:
        blocks_per_seq = seq // tm
        in_specs.append(pl.BlockSpec((None, 6, k), lambda i, j: (i // blocks_per_seq, 0, 0)))
        args.append(mod)
    in_specs.append(pl.BlockSpec((k, tn), lambda i, j: (0, j)))
    args.append(w)
    for arr, blk, imap in extras:
        in_specs.append(pl.BlockSpec(blk, imap))
        args.append(arr)
    out_specs = [pl.BlockSpec(blk, imap) for _, _, blk, imap in outs]
    out_shape = [jax.ShapeDtypeStruct(shp, dt) for shp, dt, _, _ in outs]
    scratch = [pltpu.VMEM((tm, k), BF16)] if mod is not None else []
    kern = functools.partial(_proj_kernel, has_mod=mod is not None, n_extra=len(extras),
                             n_out=len(outs), epilogue=epilogue)
    return pl.pallas_call(
        kern,
        grid=(t // tm, n // tn),
        in_specs=in_specs,
        out_specs=out_specs,
        out_shape=out_shape,
        scratch_shapes=scratch,
        compiler_params=_cparams(("parallel", "arbitrary"), vmem_mib),
        name=name,
    )(*args)


def _rope_tables(seq, dim):
    half = dim // 2
    inv_freq = ROPE_THETA ** (-jnp.arange(half, dtype=F32) * (2.0 / dim))
    ang = jnp.arange(seq, dtype=F32)[:, None] * inv_freq[None, :]
    return jnp.cos(ang), jnp.sin(ang)


def _mla_rope_tables(seq, scale):
    cos, sin = _rope_tables(seq, MLA_ROPE_DIM)
    z32 = jnp.zeros_like(cos)
    z64 = jnp.zeros((seq, LANES - MLA_ROPE_DIM), F32)
    c = jnp.concatenate([cos, cos, z64], axis=1) * scale
    s_left = jnp.concatenate([-sin, z32, z64], axis=1) * scale
    s_right = jnp.concatenate([z32, sin, z64], axis=1) * scale
    return c, s_left, s_right


def _rope64_in_chunk(r, c, s_left, s_right):
    half = MLA_ROPE_DIM // 2
    return (r * c + pltpu.roll(r, LANES - half, 1) * s_left + pltpu.roll(r, half, 1) * s_right)


def _mla_lat_epilogue(acc, j, extra, outs, *, q_rank, kv_rank):
    gq_ref, gkv_ref, c_ref, sl_ref, sr_ref = extra
    cq_ref, ckv_ref, kr_ref = outs

    def rms(v, g):
        return v * lax.rsqrt(jnp.mean(v * v, axis=-1, keepdims=True) + RMS_EPS) * g

    cq_ref[...] = rms(acc[:, :q_rank], gq_ref[...]).astype(BF16)
    ckv_ref[...] = rms(acc[:, q_rank:q_rank + kv_rank], gkv_ref[...]).astype(BF16)
    kr = acc[:, q_rank + kv_rank:]
    kr_ref[...] = _rope64_in_chunk(kr, c_ref[...], sl_ref[...], sr_ref[...]).astype(BF16)


def _mla_q_epilogue(acc, j, extra, outs, *, heads_per_tile, scale):
    c_ref, sl_ref, sr_ref = extra
    (q_ref,) = outs
    c, sl, sr = c_ref[...], sl_ref[...], sr_ref[...]
    for h in range(heads_per_tile):
        lo = h * 2 * LANES
        q_ref[:, lo:lo + LANES] = (acc[:, lo:lo + LANES] * scale).astype(BF16)
        q_ref[:, lo + LANES:lo + 2 * LANES] = _rope64_in_chunk(
            acc[:, lo + LANES:lo + 2 * LANES], c, sl, sr).astype(BF16)


def _mla_kv_kernel(ckv_ref, wkt_ref, wv_ref, kr_ref, kt_ref, v_ref, *, heads_per_tile):
    x = ckv_ref[...]
    kn_t = _dot_nt(wkt_ref[...], x)
    vv = jnp.dot(x, wv_ref[...], preferred_element_type=F32)
    kr_t = kr_ref[...].astype(F32).T.astype(BF16)
    ones = jnp.ones((x.shape[0], LANES), BF16)
    for h in range(heads_per_tile):
        lo = h * 2 * LANES
        kt_ref[lo:lo + LANES, :] = kn_t[h * LANES:(h + 1) * LANES, :].astype(BF16)
        kt_ref[lo + LANES:lo + 2 * LANES, :] = kr_t
        v_ref[:, lo:lo + LANES] = vv[:, h * LANES:(h + 1) * LANES].astype(BF16)
        v_ref[:, lo + LANES:lo + 2 * LANES] = ones


def _causal_mask(tq, tk):
    row = lax.broadcasted_iota(jnp.int32, (tq, tk), 0)
    col = lax.broadcasted_iota(jnp.int32, (tq, tk), 1)
    return col <= row


def _dot_nt(a, b):
    return lax.dot_general(a, b, (((1,), (1,)), ((), ())), preferred_element_type=F32)


def _qk(q, k_t):
    return jnp.dot(q, k_t, preferred_element_type=F32)


def _softmax_step(s, v, m_ref, l_ref, acc_ref):
    m_prev = m_ref[...]
    m_new = jnp.maximum(m_prev, jnp.max(s, axis=-1, keepdims=True))
    p = jnp.exp2(s - jnp.tile(m_new, (1, s.shape[1] // LANES)))
    alpha = jnp.exp2(m_prev - m_new)
    if l_ref is not None:
        l_ref[...] = alpha * l_ref[...] + jnp.sum(p, axis=-1, keepdims=True)
    acc_ref[...] = (acc_ref[...] * jnp.tile(alpha, (1, acc_ref.shape[1] // LANES))
                    + jnp.dot(p.astype(BF16), v, preferred_element_type=F32))
    m_ref[...] = m_new


def _mla_attn_kernel(q_ref, k_ref, v_ref, o_ref, m_ref, acc_ref, sa_ref, sb_ref, *, blk, seq, heads):
    w = 2 * LANES

    def q_body(qi, carry):
        q0 = pl.multiple_of(qi * blk, blk)
        qs = [q_ref[pl.ds(q0, blk), h * w:(h + 1) * w] for h in range(heads)]
        m_ref[...] = jnp.full(m_ref.shape, NEG_INF, F32)
        acc_ref[...] = jnp.zeros(acc_ref.shape, F32)

        def score_tile(s_ref, tile):
            k0 = pl.multiple_of(tile * blk, blk)
            keep = jnp.logical_or(_causal_mask(blk, blk), tile < qi)
            for h in range(heads):
                s = _qk(qs[h], k_ref[h * w:(h + 1) * w, pl.ds(k0, blk)])
                s_ref[h] = jnp.where(keep, s, NEG_INF)

        def consume(s_ref, tile):
            k0 = pl.multiple_of(tile * blk, blk)
            for h in range(heads):
                _softmax_step(s_ref[h], v_ref[pl.ds(k0, blk), h * w:(h + 1) * w], m_ref.at[h], None,
                              acc_ref.at[h])

        score_tile(sa_ref, 0)

        def pair_body(jj, c):
            score_tile(sb_ref, 2 * jj + 1)
            consume(sa_ref, 2 * jj)
            score_tile(sa_ref, 2 * jj + 2)
            consume(sb_ref, 2 * jj + 1)
            return c

        lax.fori_loop(0, qi // 2, pair_body, 0)

        @pl.when(qi % 2 == 1)
        def _():
            score_tile(sb_ref, qi)
            consume(sa_ref, qi - 1)
            consume(sb_ref, qi)

        @pl.when(qi % 2 == 0)
        def _():
            consume(sa_ref, qi)

        for h in range(heads):
            acc = acc_ref[h]
            o_ref[pl.ds(q0, blk), h * HEAD_DIM:(h + 1) * HEAD_DIM] = (
                acc[:, :HEAD_DIM] / acc[:, HEAD_DIM:]).astype(o_ref.dtype)
        return carry

    lax.fori_loop(0, seq // blk, q_body, 0)


def _diff_attn_kernel(q_ref, k_ref, v_ref, lam_ref, g_ref, o_ref,
                      m1_ref, l1_ref, a1_ref, m2_ref, l2_ref, a2_ref, *, blk, seq, lam_init):
    hd = HEAD_DIM
    lf = lam_ref[...]
    lam_full = (jnp.exp(jnp.sum(lf[0:1, :] * lf[1:2, :], axis=-1, keepdims=True))
                - jnp.exp(jnp.sum(lf[2:3, :] * lf[3:4, :], axis=-1, keepdims=True)) + lam_init)

    def q_body(qi, carry):
        q0 = pl.multiple_of(qi * blk, blk)
        q = q_ref[pl.ds(q0, blk), :]
        q1, q2 = q[:, :hd], q[:, hd:]
        for m_ref, l_ref, a_ref in ((m1_ref, l1_ref, a1_ref), (m2_ref, l2_ref, a2_ref)):
            m_ref[...] = jnp.full(m_ref.shape, NEG_INF, F32)
            l_ref[...] = jnp.zeros(l_ref.shape, F32)
            a_ref[...] = jnp.zeros(a_ref.shape, F32)

        def step(k0, masked):
            v = v_ref[pl.ds(k0, blk), :]
            s1 = _qk(q1, k_ref[:hd, pl.ds(k0, blk)])
            s2 = _qk(q2, k_ref[hd:, pl.ds(k0, blk)])
            if masked:
                mask = _causal_mask(blk, blk)
                s1 = jnp.where(mask, s1, NEG_INF)
                s2 = jnp.where(mask, s2, NEG_INF)
            _softmax_step(s1, v, m1_ref, l1_ref, a1_ref)
            _softmax_step(s2, v, m2_ref, l2_ref, a2_ref)

        def kv_body(kj, c):
            step(pl.multiple_of(kj * blk, blk), False)
            return c

        lax.fori_loop(0, qi, kv_body, 0)
        step(q0, True)
        reps = (1, 2 * hd // LANES)
        o = (a1_ref[...] / jnp.tile(l1_ref[...], reps)
             - lam_full * (a2_ref[...] / jnp.tile(l2_ref[...], reps)))
        o = o * lax.rsqrt(jnp.mean(o * o, axis=-1, keepdims=True) + LN_EPS) * g_ref[...]
        o_ref[pl.ds(q0, blk), :] = (o * (1.0 - lam_init)).astype(o_ref.dtype)
        return carry

    lax.fori_loop(0, seq // blk, q_body, 0)


def _sb_attn_kernel(q_ref, k_ref, v_ref, o_ref, run_ref, acc_ref, *, tq, tk, seq, heads):
    chunks_per_q = tq // tk
    hd = HEAD_DIM
    jr = lax.broadcasted_iota(jnp.int32, (2 * tk, tk), 0)
    jc = lax.broadcasted_iota(jnp.int32, (2 * tk, tk), 1)
    later_mat = (jnp.where(jr >= tk, jr - tk, jr) > jc).astype(BF16)

    def q_body(qi, carry):
        q0 = pl.multiple_of(qi * tq, tq)
        qs = [q_ref[pl.ds(q0, tq), h * hd:(h + 1) * hd] for h in range(heads)]
        run_ref[...] = jnp.zeros(run_ref.shape, F32)
        acc_ref[...] = jnp.zeros(acc_ref.shape, F32)

        def step(k0, diag_offset):
            logits = [_qk(qs[h], k_ref[h * hd:(h + 1) * hd, pl.ds(k0, tk)]) for h in range(heads)]
            for h in range(heads):
                z = logits[h]
                log_beta = jnp.minimum(z, 0.0) - jnp.log2(1.0 + jnp.exp2(-jnp.abs(z)))
                log_keep = log_beta - z
                if diag_offset is not None:
                    row = lax.broadcasted_iota(jnp.int32, (tq, tk), 0)
                    col = lax.broadcasted_iota(jnp.int32, (tq, tk), 1) + diag_offset
                    strict = col < row
                    log_keep = jnp.where(strict, log_keep, 0.0)
                hi = log_keep.astype(BF16)
                lo = (log_keep - hi.astype(F32)).astype(BF16)
                run = run_ref[h]
                later = (jnp.dot(jnp.concatenate([hi, lo], axis=1), later_mat,
                                 preferred_element_type=F32) + jnp.tile(run, (1, tk // LANES)))
                a = jnp.exp2(log_beta + later)
                if diag_offset is not None:
                    a = jnp.where(strict, a, 0.0)
                acc_ref[h] += jnp.dot(a.astype(BF16), v_ref[pl.ds(k0, tk), h * hd:(h + 1) * hd],
                                      preferred_element_type=F32)
                run_ref[h] = run + jnp.sum(log_keep, axis=-1, keepdims=True)

        for c in reversed(range(chunks_per_q)):
            step(q0 + c * tk, c * tk)

        def any_live():
            run_max = run_ref[0]
            for h in range(1, heads):
                run_max = jnp.maximum(run_max, run_ref[h])
            return jnp.max(run_max) > SB_DEAD_LOG2

        def kv_cond(carry):
            n, live = carry
            return jnp.logical_and(n < qi * chunks_per_q, live)

        def kv_body(carry):
            n, _ = carry
            kj = qi * chunks_per_q - 1 - n
            step(pl.multiple_of(kj * tk, tk), None)
            return n + 1, any_live()

        lax.while_loop(kv_cond, kv_body, (jnp.int32(0), any_live()))
        for h in range(heads):
            o_ref[pl.ds(q0, tq), h * hd:(h + 1) * hd] = acc_ref[h].astype(o_ref.dtype)
        return carry

    lax.fori_loop(0, seq // tq, q_body, 0)


def _attention(kernel, q, k, v, *, batch, seq, groups, k_off, v_off, dq, dv, d_out, scratch,
               vmem_mib, name):
    in_specs = [
        pl.BlockSpec((seq, dq), lambda b, g: (b, g)),
        pl.BlockSpec((dq, seq), lambda b, g: (k_off + g, b)),
        pl.BlockSpec((seq, dv), lambda b, g: (b, v_off + g)),
    ]
    return pl.pallas_call(
        kernel,
        grid=(batch, groups),
        in_specs=in_specs,
        out_specs=pl.BlockSpec((seq, d_out), lambda b, g: (b, g)),
        out_shape=jax.ShapeDtypeStruct((batch * seq, groups * d_out), BF16),
        scratch_shapes=scratch,
        compiler_params=_cparams(("parallel", "parallel"), vmem_mib),
        name=name,
    )(q, k, v)


def _mla_mixer(x2d, mod, seq, batch, w_in, q_norm_g, kv_norm_g, w_uq, w_ukv):
    t, d = x2d.shape
    q_rank, kv_rank = q_norm_g.shape[0], kv_norm_g.shape[0]
    heads = w_uq.shape[1] // (HEAD_DIM + MLA_ROPE_DIM)
    scale = (HEAD_DIM + MLA_ROPE_DIM) ** -0.5 * LOG2_E
    tm = min(512, seq)
    row_blocks = seq // tm

    lat_n = q_rank + kv_rank + LANES
    w_in_p = jnp.zeros((d, lat_n), BF16).at[:, :w_in.shape[1]].set(w_in.astype(BF16))
    c_k, sl_k, sr_k = _mla_rope_tables(seq, 1.0)
    c_q, sl_q, sr_q = _mla_rope_tables(seq, scale)
    tab_spec = ((tm, LANES), lambda i, j: (i % row_blocks, 0))

    cq, ckv, kr = _proj(
        x2d, w_in_p, tm=tm, tn=lat_n, mod=mod, seq=seq,
        extras=[(q_norm_g.reshape(1, q_rank), (1, q_rank), lambda i, j: (0, 0)),
                (kv_norm_g.reshape(1, kv_rank), (1, kv_rank), lambda i, j: (0, 0)),
                (c_k,) + tab_spec, (sl_k,) + tab_spec, (sr_k,) + tab_spec],
        outs=[((t, q_rank), BF16, (tm, q_rank), lambda i, j: (i, 0)),
              ((t, kv_rank), BF16, (tm, kv_rank), lambda i, j: (i, 0)),
              ((t, LANES), BF16, (tm, LANES), lambda i, j: (i, 0))],
        epilogue=functools.partial(_mla_lat_epilogue, q_rank=q_rank, kv_rank=kv_rank),
        name="mla_latent")

    w_q = w_uq.reshape(q_rank, heads, HEAD_DIM + MLA_ROPE_DIM).astype(BF16)
    w_q = jnp.pad(w_q, ((0, 0), (0, 0), (0, 2 * LANES - HEAD_DIM - MLA_ROPE_DIM)))
    w_q = w_q.reshape(q_rank, heads * 2 * LANES)
    hpt = min(heads, 4)
    (q_cat,) = _proj(
        cq, w_q, tm=tm, tn=hpt * 2 * LANES, mod=None, seq=seq,
        extras=[(c_q,) + tab_spec, (sl_q,) + tab_spec, (sr_q,) + tab_spec],
        outs=[((t, heads * 2 * LANES), BF16, (tm, hpt * 2 * LANES), lambda i, j: (i, j))],
        epilogue=functools.partial(_mla_q_epilogue, heads_per_tile=hpt, scale=scale),
        name="mla_q_up")

    w_kv = w_ukv.reshape(kv_rank, heads, 2 * HEAD_DIM).astype(BF16)
    w_kt = w_kv[:, :, :HEAD_DIM].reshape(kv_rank, heads * HEAD_DIM).T
    w_v = w_kv[:, :, HEAD_DIM:].reshape(kv_rank, heads * HEAD_DIM)
    k_cat, v = pl.pallas_call(
        functools.partial(_mla_kv_kernel, heads_per_tile=hpt),
        grid=(t // tm, heads // hpt),
        in_specs=[
            pl.BlockSpec((tm, kv_rank), lambda i, j: (i, 0)),
            pl.BlockSpec((hpt * HEAD_DIM, kv_rank), lambda i, j: (j, 0)),
            pl.BlockSpec((kv_rank, hpt * HEAD_DIM), lambda i, j: (0, j)),
            pl.BlockSpec((tm, LANES), lambda i, j: (i, 0)),
        ],
        out_specs=[
            pl.BlockSpec((hpt * 2 * LANES, tm), lambda i, j: (j, i)),
            pl.BlockSpec((tm, hpt * 2 * LANES), lambda i, j: (i, j)),
        ],
        out_shape=[jax.ShapeDtypeStruct((heads * 2 * LANES, t), BF16),
                   jax.ShapeDtypeStruct((t, heads * 2 * LANES), BF16)],
        compiler_params=_cparams(("parallel", "parallel"), 32),
        name="mla_kv_up",
    )(ckv, w_kt, w_v, kr)

    blk = min(512, seq)
    hg = ATTN_HEAD_GROUP
    return _attention(
        functools.partial(_mla_attn_kernel, blk=blk, seq=seq, heads=hg), q_cat, k_cat, v,
        batch=batch, seq=seq, groups=heads // hg, k_off=0, v_off=0,
        dq=hg * 2 * LANES, dv=hg * 2 * LANES, d_out=hg * HEAD_DIM,
        scratch=[pltpu.VMEM((hg, blk, LANES), F32), pltpu.VMEM((hg, blk, 2 * LANES), F32),
                 pltpu.VMEM((hg, blk, blk), F32), pltpu.VMEM((hg, blk, blk), F32)],
        vmem_mib=48, name="mla_attention")


def _qkv_out_specs(t, width, tm, tn):
    nq = width // tn

    def qv_map(i, j):
        return i, jnp.where(j < nq, j, jnp.where(j < 2 * nq, nq - 1, j - nq))

    def kt_map(i, j):
        return jnp.clip(j - nq, 0, nq - 1), i

    return [((t, 2 * width), BF16, (tm, tn), qv_map), ((width, t), BF16, (tn, tm), kt_map)]


def _diff_qkv_epilogue(acc, j, extra, outs, *, tn, width, scale):
    cos_ref, sin_ref = extra
    qv_ref, kt_ref = outs

    def rope(factor):
        cos = cos_ref[...] * factor
        sin = sin_ref[...] * factor
        chunks = []
        for c in range(tn // LANES):
            a = acc[:, c * LANES:(c + 1) * LANES]
            chunks.append(a * cos + pltpu.roll(a, HEAD_DIM // 2, 1) * sin)
        return jnp.concatenate(chunks, axis=1)

    @pl.when(j * tn < width)
    def _():
        qv_ref[...] = rope(scale).astype(BF16)

    @pl.when(jnp.logical_and(j * tn >= width, j * tn < 2 * width))
    def _():
        kt_ref[...] = rope(1.0).T.astype(BF16)

    @pl.when(j * tn >= 2 * width)
    def _():
        qv_ref[...] = acc.astype(BF16)


def _diff_mixer(x2d, mod, seq, batch, layer_idx, w_qkv, lam, subln_g):
    t, d = x2d.shape
    width = w_qkv.shape[1] // 3
    heads = width // (2 * HEAD_DIM)
    scale = HEAD_DIM ** -0.5 * LOG2_E
    lam_init = 0.8 - 0.6 * math.exp(-0.3 * layer_idx)
    tm = min(512, seq)
    tn = min(1024, width)
    row_blocks = seq // tm
    cos, sin = _rope_tables(seq, HEAD_DIM)
    cos_t = jnp.concatenate([cos, cos], axis=1)
    sin_t = jnp.concatenate([-sin, sin], axis=1)
    tab_spec = ((tm, LANES), lambda i, j: (i % row_blocks, 0))
    qv, k_t = _proj(
        x2d, w_qkv.astype(BF16), tm=tm, tn=tn, mod=mod, seq=seq,
        extras=[(cos_t,) + tab_spec, (sin_t,) + tab_spec],
        outs=_qkv_out_specs(t, width, tm, tn),
        epilogue=functools.partial(_diff_qkv_epilogue, tn=tn, width=width, scale=scale),
        name="diff_qkv")
    blk = min(512, seq)
    dh = 2 * HEAD_DIM
    in_specs = [
        pl.BlockSpec((seq, dh), lambda b, h: (b, h)),
        pl.BlockSpec((dh, seq), lambda b, h: (h, b)),
        pl.BlockSpec((seq, dh), lambda b, h: (b, heads + h)),
        pl.BlockSpec((4, HEAD_DIM), lambda b, h: (0, 0)),
        pl.BlockSpec((1, dh), lambda b, h: (0, 0)),
    ]
    stat = lambda: pltpu.VMEM((blk, LANES), F32)
    accum = lambda: pltpu.VMEM((blk, dh), F32)
    return pl.pallas_call(
        functools.partial(_diff_attn_kernel, blk=blk, seq=seq, lam_init=lam_init),
        grid=(batch, heads),
        in_specs=in_specs,
        out_specs=pl.BlockSpec((seq, dh), lambda b, h: (b, h)),
        out_shape=jax.ShapeDtypeStruct((t, width), BF16),
        scratch_shapes=[stat(), stat(), accum(), stat(), stat(), accum()],
        compiler_params=_cparams(("parallel", "parallel"), 40),
        name="diff_attention",
    )(qv, k_t, qv, lam, subln_g.reshape(1, dh))


def _sb_qkv_epilogue(acc, j, extra, outs, *, tn, width, scale):
    qv_ref, kt_ref = outs

    @pl.when(j * tn < width)
    def _():
        qv_ref[...] = (acc * scale).astype(BF16)

    @pl.when(jnp.logical_and(j * tn >= width, j * tn < 2 * width))
    def _():
        kt_ref[...] = acc.T.astype(BF16)

    @pl.when(j * tn >= 2 * width)
    def _():
        qv_ref[...] = acc.astype(BF16)


def _sb_mixer(x2d, mod, seq, batch, w_qkv):
    t, d = x2d.shape
    width = w_qkv.shape[1] // 3
    heads = width // HEAD_DIM
    scale = HEAD_DIM ** -0.5 * LOG2_E
    tm = min(512, seq)
    tn = min(1024, width)
    qv, k_t = _proj(
        x2d, w_qkv.astype(BF16), tm=tm, tn=tn, mod=mod, seq=seq, extras=[],
        outs=_qkv_out_specs(t, width, tm, tn),
        epilogue=functools.partial(_sb_qkv_epilogue, tn=tn, width=width, scale=scale),
        name="sb_qkv")
    tq = min(512, seq)
    tk = min(256, seq)
    hg = ATTN_HEAD_GROUP
    groups = heads // hg
    return _attention(
        functools.partial(_sb_attn_kernel, tq=tq, tk=tk, seq=seq, heads=hg), qv, k_t, qv,
        batch=batch, seq=seq, groups=groups, k_off=0, v_off=groups,
        dq=hg * HEAD_DIM, dv=hg * HEAD_DIM, d_out=hg * HEAD_DIM,
        scratch=[pltpu.VMEM((hg, tq, LANES), F32), pltpu.VMEM((hg, tq, HEAD_DIM), F32)],
        vmem_mib=40, name="sb_attention")


def _split_bf16(v):
    hi = v.astype(BF16)
    return hi, (v - hi.astype(F32)).astype(BF16)


def _outproj_router_kernel(o_ref, w_ref, x_ref, mod_ref, g_ref, b_ref, rw_ref, rb_ref,
                           x1_ref, idx_ref, gate_ref, *, alpha, n_experts):
    y = jnp.dot(o_ref[...], w_ref[...], preferred_element_type=F32)
    mod = mod_ref[...]
    x1 = _layer_norm_rows(alpha * x_ref[...] + mod[2:3, :] * y, g_ref[...], b_ref[...])
    x1_ref[...] = x1
    h2 = x1 * (1.0 + mod[4:5, :]) + mod[3:4, :]
    hi, lo = _split_bf16(h2)
    logits = jnp.dot(jnp.concatenate([hi, lo, hi], axis=1), rw_ref[...],
                     preferred_element_type=F32) + rb_ref[...]
    lane = lax.broadcasted_iota(jnp.int32, logits.shape, 1).astype(F32)
    work = jnp.where(lane < n_experts, logits, -jnp.inf)
    idx_out = jnp.zeros(logits.shape, F32)
    val_out = jnp.zeros(logits.shape, F32)
    top = None
    denom = None
    for kk in range(TOP_K):
        m = jnp.max(work, axis=-1, keepdims=True)
        first = jnp.min(jnp.where(work == m, lane, float(LANES)), axis=-1, keepdims=True)
        if kk == 0:
            top = m
        e = jnp.exp(m - top)
        denom = e if kk == 0 else denom + e
        idx_out = jnp.where(lane == kk, first, idx_out)
        val_out = jnp.where(lane == kk, e, val_out)
        work = jnp.where(lane == first, -jnp.inf, work)
    idx_ref[...] = idx_out.astype(jnp.int32)
    gate_ref[...] = val_out / denom


def _outproj_router(o, w_o, x2d, mod, ln_g, ln_b, router_w, router_b, *, seq, alpha):
    t, d = x2d.shape
    n_experts = router_w.shape[1]
    tm = min(256, seq)
    blocks_per_seq = seq // tm
    once = pl.Buffered(1)
    rw = jnp.zeros((d, LANES), F32).at[:, :n_experts].set(router_w)
    rw_hi, rw_lo = _split_bf16(rw)
    rw3 = jnp.concatenate([rw_hi, rw_hi, rw_lo], axis=0)
    rb = jnp.zeros((1, LANES), F32).at[0, :n_experts].set(router_b)
    row = lambda i: (i, 0)
    fixed = lambda i: (0, 0)
    return pl.pallas_call(
        functools.partial(_outproj_router_kernel, alpha=alpha, n_experts=n_experts),
        grid=(t // tm,),
        in_specs=[
            pl.BlockSpec((tm, o.shape[1]), row),
            pl.BlockSpec(w_o.shape, fixed, pipeline_mode=once),
            pl.BlockSpec((tm, d), row),
            pl.BlockSpec((None, 6, d), lambda i: (i // blocks_per_seq, 0, 0)),
            pl.BlockSpec((1, d), fixed),
            pl.BlockSpec((1, d), fixed),
            pl.BlockSpec((3 * d, LANES), fixed, pipeline_mode=once),
            pl.BlockSpec((1, LANES), fixed),
        ],
        out_specs=[pl.BlockSpec((tm, d), row),
                   pl.BlockSpec((tm, LANES), row), pl.BlockSpec((tm, LANES), row)],
        out_shape=[jax.ShapeDtypeStruct((t, d), F32),
                   jax.ShapeDtypeStruct((t, LANES), jnp.int32), jax.ShapeDtypeStruct((t, LANES), F32)],
        compiler_params=_cparams(("parallel",), 48),
        name="outproj_ln_router",
    )(o, w_o.astype(BF16), x2d, mod, ln_g.reshape(1, d), ln_b.reshape(1, d), rw3, rb)


def _plan_kernel(idx_ref, rank_ref, cnt_ref, run_ref):
    @pl.when(pl.program_id(0) == 0)
    def _():
        run_ref[...] = jnp.zeros(run_ref.shape, F32)

    idx = idx_ref[...]
    tb = idx.shape[0]
    lane = lax.broadcasted_iota(jnp.int32, idx.shape, 1)
    onehots = [(lane == idx[:, kk:kk + 1]).astype(F32) for kk in range(TOP_K)]
    chosen = onehots[0]
    for kk in range(1, TOP_K):
        chosen = chosen + onehots[kk]
    earlier = (lax.broadcasted_iota(jnp.int32, (tb, tb), 1)
               < lax.broadcasted_iota(jnp.int32, (tb, tb), 0)).astype(BF16)
    prefix = jnp.dot(earlier, chosen.astype(BF16), preferred_element_type=F32) + run_ref[0:1, :]
    rank = jnp.zeros(idx.shape, F32)
    for kk in range(TOP_K):
        rank = jnp.where(lane == kk, jnp.sum(prefix * onehots[kk], axis=-1, keepdims=True), rank)
    rank_ref[...] = rank.astype(jnp.int32)
    run_ref[...] = run_ref[...] + jnp.sum(chosen, axis=0, keepdims=True)
    cnt_ref[...] = run_ref[...]


def _routing_plan(idx_pad, n_experts, tm):
    t = idx_pad.shape[0]
    tb = min(512, t)
    rank_pad, cnt = pl.pallas_call(
        _plan_kernel,
        grid=(t // tb,),
        in_specs=[pl.BlockSpec((tb, LANES), lambda i: (i, 0))],
        out_specs=[pl.BlockSpec((tb, LANES), lambda i: (i, 0)),
                   pl.BlockSpec((SUBLANES, LANES), lambda i: (0, 0))],
        out_shape=[jax.ShapeDtypeStruct((t, LANES), jnp.int32),
                   jax.ShapeDtypeStruct((SUBLANES, LANES), F32)],
        scratch_shapes=[pltpu.VMEM((SUBLANES, LANES), F32)],
        compiler_params=_cparams(("arbitrary",), 16),
        name="moe_plan",
    )(idx_pad)
    counts = cnt[0, :n_experts].astype(jnp.int32)
    padded = (counts + tm - 1) // tm * tm
    pad_end = jnp.cumsum(padded)
    pad_start = pad_end - padded
    chose = idx_pad[:, :TOP_K, None] == jnp.arange(n_experts, dtype=jnp.int32)
    dest = (jnp.sum(jnp.where(chose, pad_start, 0), axis=-1) + rank_pad[:, :TOP_K]).astype(jnp.int32)
    n_blocks = t * TOP_K // tm + n_experts
    block_start = jnp.arange(n_blocks, dtype=jnp.int32) * tm
    block_expert = jnp.minimum(jnp.sum(block_start[:, None] >= pad_end[None, :], axis=1),
                               n_experts - 1).astype(jnp.int32)
    n_used = (pad_end[-1] // tm).astype(jnp.int32).reshape(1)
    fill_start = jnp.concatenate([pad_start + counts, pad_end[-1:]]).astype(jnp.int32)
    fill_count = jnp.concatenate([pad_end - pad_start - counts,
                                  n_blocks * tm - pad_end[-1:]]).astype(jnp.int32)
    prev_expert = jnp.concatenate([jnp.full((1,), -1, jnp.int32), block_expert[:-1]])
    run_first = (block_expert != prev_expert).astype(jnp.int32)
    ids = jnp.arange(n_experts, dtype=jnp.int32)
    later_nonempty = jnp.logical_and(ids[None, :] > ids[:, None], counts[None, :] > 0)
    next_nonempty = jnp.min(jnp.where(later_nonempty, ids[None, :], n_experts), axis=1)
    run_next = jnp.where(next_nonempty < n_experts, next_nonempty, -1).astype(jnp.int32)[block_expert]
    return dict(dest=dest.reshape(-1), block_expert=block_expert, n_used=n_used, fill_start=fill_start,
                fill_count=fill_count, n_blocks=n_blocks, run_first=run_first, run_next=run_next)


def _row_pitch(chunks):
    return chunks + 1


def _pack_pairs(v):
    w = v.shape[1] // 2
    hi = lax.bitcast_convert_type(v[:, :w].astype(BF16).astype(F32), jnp.uint32)
    lo = lax.bitcast_convert_type(v[:, w:].astype(BF16).astype(F32), jnp.uint32)
    return hi | (lo >> 16)


def _unpack_pairs(u):
    hi = lax.bitcast_convert_type(u & jnp.uint32(0xFFFF0000), F32)
    lo = lax.bitcast_convert_type(u << 16, F32)
    return jnp.concatenate([hi, lo], axis=1)


def _load_rows(ref, lead, n_rows, chunks):
    pitch = _row_pitch(chunks)
    return jnp.concatenate(
        [ref[lead + (pl.ds(c, n_rows, stride=pitch), slice(None))] for c in range(chunks)], axis=1)


def _store_rows(ref, lead, val, chunks):
    pitch = _row_pitch(chunks)
    n_rows = val.shape[0]
    for c in range(chunks):
        ref[lead + (pl.ds(c, n_rows, stride=pitch), slice(None))] = val[:, c * LANES:(c + 1) * LANES]
    ref[lead + (pl.ds(chunks, n_rows, stride=pitch), slice(None))] = jnp.zeros((n_rows, LANES), val.dtype)


def _dispatch_kernel(dest_ref, fstart_ref, fcount_ref, x_ref, mod_ref, xs_hbm, hbuf, zrow, sem, fsem,
                     *, tb, n_steps, n_fills, chunks):
    i = pl.program_id(0)
    slot = i % 2
    pitch = _row_pitch(chunks)

    def drain(buf_slot):
        for _ in range(TOP_K):
            pltpu.make_async_copy(hbuf.at[buf_slot], xs_hbm.at[pl.ds(0, tb * pitch)],
                                  sem.at[buf_slot]).wait()

    def for_each_fill(fn):
        def per_range(f, c):
            start = fstart_ref[f]

            def per_row(j, c2):
                fn(pltpu.make_async_copy(zrow, xs_hbm.at[pl.ds((start + j) * pitch, pitch)], fsem))
                return c2

            lax.fori_loop(0, fcount_ref[f], per_row, 0)
            return c

        lax.fori_loop(0, n_fills, per_range, 0)

    @pl.when(i == 0)
    def _():
        zrow[...] = jnp.zeros(zrow.shape, zrow.dtype)
        for_each_fill(lambda copy: copy.start())

    @pl.when(i >= 2)
    def _():
        drain(slot)

    mod = mod_ref[...]
    h2 = x_ref[...] * (1.0 + mod[4:5, :]) + mod[3:4, :]
    _store_rows(hbuf, (slot,), _pack_pairs(h2), chunks)

    def per_token(r, c):
        base = (i * tb + r) * TOP_K
        src = hbuf.at[slot, pl.ds(r * pitch, pitch)]
        for kk in range(TOP_K):
            pltpu.make_async_copy(src, xs_hbm.at[pl.ds(dest_ref[base + kk] * pitch, pitch)],
                                  sem.at[slot]).start()
        return c

    lax.fori_loop(0, tb, per_token, 0, unroll=4)

    @pl.when(i == n_steps - 1)
    def _():
        drain(slot)
        if n_steps > 1:
            drain(1 - slot)
        for_each_fill(lambda copy: copy.wait())


def _dispatch(x1, mod, plan, *, seq, tm):
    dest, fill_start, fill_count = plan["dest"], plan["fill_start"], plan["fill_count"]
    n_blocks = plan["n_blocks"]
    t, d = x1.shape
    chunks = d // (2 * LANES)
    pitch = _row_pitch(chunks)
    tb = min(256, seq)
    n_steps = t // tb
    blocks_per_seq = seq // tb
    grid_spec = pltpu.PrefetchScalarGridSpec(
        num_scalar_prefetch=3,
        grid=(n_steps,),
        in_specs=[
            pl.BlockSpec((tb, d), lambda i, *_: (i, 0)),
            pl.BlockSpec((None, 6, d), lambda i, *_: (i // blocks_per_seq, 0, 0)),
        ],
        out_specs=pl.BlockSpec(memory_space=pl.ANY),
        scratch_shapes=[pltpu.VMEM((2, tb * pitch, LANES), jnp.uint32),
                        pltpu.VMEM((pitch, LANES), jnp.uint32),
                        pltpu.SemaphoreType.DMA((2,)), pltpu.SemaphoreType.DMA(())],
    )
    return pl.pallas_call(
        functools.partial(_dispatch_kernel, tb=tb, n_steps=n_steps, n_fills=fill_start.shape[0],
                          chunks=chunks),
        grid_spec=grid_spec,
        out_shape=jax.ShapeDtypeStruct((n_blocks * tm * pitch, LANES), jnp.uint32),
        compiler_params=_cparams(("arbitrary",), 32),
        name="moe_dispatch",
    )(dest, fill_start, fill_count, x1, mod)


def _expert_kernel(be_ref, used_ref, first_ref, next_ref, xs_ref, wgu_hbm, bgu_ref, wd_hbm, bd_ref,
                   y_ref, gu_stage, dn_stage, gu_bf16, dn_bf16, sem, *, layer, tm, d_expert, chunks):
    i = pl.program_id(0)

    def weight_copies(expert):
        return (pltpu.make_async_copy(wgu_hbm.at[layer, expert], gu_stage, sem.at[0]),
                pltpu.make_async_copy(wd_hbm.at[layer, expert], dn_stage, sem.at[1]))

    def cast(stage, out):
        n_rows = stage.shape[0]
        slab = math.gcd(n_rows, 256)

        def body(j, c):
            rows = pl.ds(pl.multiple_of(j * slab, slab), slab)
            out[rows, :] = stage[rows, :].astype(BF16)
            return c
        lax.fori_loop(0, n_rows // slab, body, 0)

    @pl.when(i == 0)
    def _():
        for copy in weight_copies(be_ref[0]):
            copy.start()

    @pl.when(jnp.logical_and(i < used_ref[0], first_ref[i] == 1))
    def _():
        for copy in weight_copies(be_ref[i]):
            copy.wait()
        cast(gu_stage, gu_bf16)
        cast(dn_stage, dn_bf16)

        @pl.when(next_ref[i] >= 0)
        def _():
            for copy in weight_copies(next_ref[i]):
                copy.start()

    @pl.when(i < used_ref[0])
    def _():
        x = _unpack_pairs(_load_rows(xs_ref, (), tm, chunks)).astype(BF16)
        gu = jnp.dot(x, gu_bf16[...], preferred_element_type=F32) + bgu_ref[...]
        g = jnp.minimum(gu[:, :d_expert], SWIGLU_LIMIT)
        lin = jnp.clip(gu[:, d_expert:], -SWIGLU_LIMIT, SWIGLU_LIMIT)
        act = g * jax.nn.sigmoid(SWIGLU_ALPHA * g) * (lin + 1.0)
        y = jnp.dot(act.astype(BF16), dn_bf16[...], preferred_element_type=F32) + bd_ref[...]
        _store_rows(y_ref, (), _pack_pairs(y), chunks)

    @pl.when(i >= used_ref[0])
    def _():
        y_ref[...] = jnp.zeros(y_ref.shape, y_ref.dtype)


def _experts(xs, plan, layer, w_gate_up, b_gate_up, w_down, b_down, tm):
    n_blocks = plan["n_blocks"]
    depth, n_experts, d_expert, d = w_down.shape
    chunks = d // (2 * LANES)
    pitch = _row_pitch(chunks)
    per_expert = lambda i, be, used, first, nxt: (layer, be[i], 0, 0)
    rows = lambda i, be, used, first, nxt: (i, 0)
    used_rows = lambda i, be, used, first, nxt: (jnp.minimum(i, used[0] - 1), 0)
    grid_spec = pltpu.PrefetchScalarGridSpec(
        num_scalar_prefetch=4,
        grid=(n_blocks,),
        in_specs=[
            pl.BlockSpec((tm * pitch, LANES), used_rows),
            pl.BlockSpec(memory_space=pl.ANY),
            pl.BlockSpec((None, None, 1, 2 * d_expert), per_expert),
            pl.BlockSpec(memory_space=pl.ANY),
            pl.BlockSpec((None, None, 1, d), per_expert),
        ],
        out_specs=pl.BlockSpec((tm * pitch, LANES), rows),
        scratch_shapes=[pltpu.VMEM((d, 2 * d_expert), F32), pltpu.VMEM((d_expert, d), F32),
                        pltpu.VMEM((d, 2 * d_expert), BF16), pltpu.VMEM((d_expert, d), BF16),
                        pltpu.SemaphoreType.DMA((2,))],
    )
    return pl.pallas_call(
        functools.partial(_expert_kernel, layer=layer, tm=tm, d_expert=d_expert, chunks=chunks),
        grid_spec=grid_spec,
        out_shape=jax.ShapeDtypeStruct((n_blocks * tm * pitch, LANES), jnp.uint32),
        compiler_params=_cparams(("arbitrary",), 56),
        name="moe_experts",
    )(plan["block_expert"], plan["n_used"], plan["run_first"], plan["run_next"], xs, w_gate_up,
      b_gate_up.reshape(depth, n_experts, 1, 2 * d_expert), w_down,
      b_down.reshape(depth, n_experts, 1, d))


def _combine_kernel(dest_ref, ys_hbm, x_ref, gate_ref, mod_ref, g_ref, b_ref, o_ref, ybuf, sem,
                    *, tn, n_blocks, alpha, chunks):
    i = pl.program_id(0)
    slot = i % 2
    pitch = _row_pitch(chunks)

    def issue(block, buf_slot):
        def body(r, c):
            base = (block * tn + r) * TOP_K
            for kk in range(TOP_K):
                pltpu.make_async_copy(ys_hbm.at[pl.ds(dest_ref[base + kk] * pitch, pitch)],
                                      ybuf.at[buf_slot, kk, pl.ds(r * pitch, pitch)],
                                      sem.at[buf_slot]).start()
            return c
        lax.fori_loop(0, tn, body, 0, unroll=4)

    @pl.when(i == 0)
    def _():
        issue(0, 0)

    @pl.when(i + 1 < n_blocks)
    def _():
        issue(i + 1, 1 - slot)

    for kk in range(TOP_K):
        pltpu.make_async_copy(ys_hbm.at[pl.ds(0, tn * pitch)], ybuf.at[slot, kk], sem.at[slot]).wait()

    gate = gate_ref[...]
    y = None
    for kk in range(TOP_K):
        term = gate[:, kk:kk + 1] * _unpack_pairs(_load_rows(ybuf, (slot, kk), tn, chunks))
        y = term if y is None else y + term
    mod = mod_ref[...]
    o_ref[...] = _layer_norm_rows(alpha * x_ref[...] + mod[5:6, :] * y, g_ref[...], b_ref[...])


def _combine(ys, dest, x1, gate_pad, mod, ln_g, ln_b, *, seq, alpha):
    t, d = x1.shape
    chunks = d // (2 * LANES)
    pitch = _row_pitch(chunks)
    tn = min(128, seq)
    n_blocks = t // tn
    blocks_per_seq = seq // tn
    grid_spec = pltpu.PrefetchScalarGridSpec(
        num_scalar_prefetch=1,
        grid=(n_blocks,),
        in_specs=[
            pl.BlockSpec(memory_space=pl.ANY),
            pl.BlockSpec((tn, d), lambda i, dst: (i, 0)),
            pl.BlockSpec((tn, LANES), lambda i, dst: (i, 0)),
            pl.BlockSpec((None, 6, d), lambda i, dst: (i // blocks_per_seq, 0, 0)),
            pl.BlockSpec((1, d), lambda i, dst: (0, 0)),
            pl.BlockSpec((1, d), lambda i, dst: (0, 0)),
        ],
        out_specs=pl.BlockSpec((tn, d), lambda i, dst: (i, 0)),
        scratch_shapes=[pltpu.VMEM((2, TOP_K, tn * pitch, LANES), jnp.uint32), pltpu.SemaphoreType.DMA((2,))],
    )
    return pl.pallas_call(
        functools.partial(_combine_kernel, tn=tn, n_blocks=n_blocks, alpha=alpha, chunks=chunks),
        grid_spec=grid_spec,
        out_shape=jax.ShapeDtypeStruct((t, d), F32),
        compiler_params=_cparams(("arbitrary",), 40),
        name="moe_combine_ln",
    )(dest, ys, x1, gate_pad, mod, ln_g.reshape(1, d), ln_b.reshape(1, d))


def kernel(x, c, ada_w, ada_b, ln_g, ln_b, mla_w_in, mla_q_norm_g, mla_kv_norm_g, mla_w_uq, mla_w_ukv,
           mla_w_o, diff_w_qkv, diff_lambda, diff_subln_g, diff_w_o, sb_w_qkv, sb_w_o, moe_router_w,
           moe_router_b, moe_w_gate_up, moe_b_gate_up, moe_w_down, moe_b_down):
    batch, seq, d = x.shape
    depth = ada_w.shape[0]
    alpha = (2 * depth) ** 0.25
    n_experts = moe_router_w.shape[-1]
    moe_tm = 256
    mods = _adaln(c, ada_w, ada_b)
    x2d = x.reshape(batch * seq, d)
    for i in range(depth):
        mod = mods[i]
        kind, j = i % N_MIXERS, i // N_MIXERS
        if kind == 0:
            o = _mla_mixer(x2d, mod, seq, batch, mla_w_in[j], mla_q_norm_g[j], mla_kv_norm_g[j],
                           mla_w_uq[j], mla_w_ukv[j])
            w_o = mla_w_o[j]
        elif kind == 1:
            o = _diff_mixer(x2d, mod, seq, batch, i, diff_w_qkv[j], diff_lambda[j], diff_subln_g[j])
            w_o = diff_w_o[j]
        else:
            o = _sb_mixer(x2d, mod, seq, batch, sb_w_qkv[j])
            w_o = sb_w_o[j]
        x1, idx_pad, gate_pad = _outproj_router(
            o, w_o, x2d, mod, ln_g[i, 0], ln_b[i, 0], moe_router_w[i], moe_router_b[i],
            seq=seq, alpha=alpha)
        plan = _routing_plan(idx_pad, n_experts, moe_tm)
        xs = _dispatch(x1, mod, plan, seq=seq, tm=moe_tm)
        ys = _experts(xs, plan, i, moe_w_gate_up, moe_b_gate_up, moe_w_down, moe_b_down, moe_tm)
        x2d = _combine(ys, plan["dest"], x1, gate_pad, mod, ln_g[i, 1], ln_b[i, 1], seq=seq,
                       alpha=alpha)
    return x2d.reshape(batch, seq, d)
```

```python
import functools
import math

import jax
import jax.numpy as jnp
from jax import lax
from jax.experimental import pallas as pl
from jax.experimental.pallas import tpu as pltpu

F32 = jnp.float32
BF16 = jnp.bfloat16

N_MIXERS = 3
NEG_INF = -1e30
ROPE_THETA = 10000.0
LN_EPS = 1e-5
RMS_EPS = 1e-6
HEAD_DIM = 128
MLA_ROPE_DIM = 64
TOP_K = 4
SWIGLU_LIMIT = 7.0
SWIGLU_ALPHA = 1.702
LANES = 128
SUBLANES = 8
LOG2_E = 1.4426950408889634
ATTN_HEAD_GROUP = 2
PROJ_PART = 512
SB_DEAD_LOG2 = -160.0
MIB = 1024 * 1024


def _cparams(sems, vmem_mib):
    return pltpu.CompilerParams(dimension_semantics=sems, vmem_limit_bytes=vmem_mib * MIB)


def _layer_norm_rows(r, g, b):
    mu = jnp.mean(r, axis=-1, keepdims=True)
    d = r - mu
    var = jnp.mean(d * d, axis=-1, keepdims=True)
    return d * lax.rsqrt(var + LN_EPS) * g + b


def _adaln_kernel(c_ref, w_ref, b_ref, o_ref):
    c = c_ref[...]
    cond = c * jax.nn.sigmoid(c)
    o_ref[...] = jnp.dot(cond.astype(BF16), w_ref[...].astype(BF16),
                         preferred_element_type=F32) + b_ref[...]


def _adaln(c, ada_w, ada_b):
    depth, d, n = ada_w.shape
    b = c.shape[0]
    rows = 16
    tn = next(cand for cand in (1024, 512, 256, 128) if n % cand == 0)
    c_pad = jnp.zeros((rows, d), F32).at[:b].set(c)
    out = pl.pallas_call(
        _adaln_kernel,
        grid=(depth, n // tn),
        in_specs=[
            pl.BlockSpec((rows, d), lambda l, j: (0, 0)),
            pl.BlockSpec((None, d, tn), lambda l, j: (l, 0, j)),
            pl.BlockSpec((None, 1, tn), lambda l, j: (l, 0, j)),
        ],
        out_specs=pl.BlockSpec((None, rows, tn), lambda l, j: (l, 0, j)),
        out_shape=jax.ShapeDtypeStruct((depth, rows, n), F32),
        compiler_params=_cparams(("parallel", "parallel"), 40),
        name="adaln",
    )(c_pad, ada_w, ada_b.reshape(depth, 1, n))
    return out[:, :b].reshape(depth, b, 6, d)


def _proj_kernel(*refs, has_mod, n_extra, n_out, epilogue):
    x_ref = refs[0]
    pos = 1
    mod_ref = None
    if has_mod:
        mod_ref = refs[pos]
        pos += 1
    w_ref = refs[pos]
    pos += 1
    extra = refs[pos:pos + n_extra]
    pos += n_extra
    outs = refs[pos:pos + n_out]
    pos += n_out
    if has_mod:
        h_ref = refs[pos]

        @pl.when(pl.program_id(1) == 0)
        def _():
            mod = mod_ref[...]
            h_ref[...] = (x_ref[...] * (1.0 + mod[1:2, :]) + mod[0:1, :]).astype(BF16)

        h = h_ref[...]
    else:
        h = x_ref[...]
    def matmul(c0, c1):
        return jnp.dot(h, w_ref[:, c0:c1], preferred_element_type=F32)

    matmul.transposed = lambda w_t: _dot_nt(w_t, h)
    epilogue(matmul, pl.program_id(1), extra, outs)


def _proj(x, w, *, tm, tn, mod, seq, extras, outs, epilogue, vmem_mib=48, name):
    t, k = x.shape
    n = w.shape[1]
    in_specs = [pl.BlockSpec((tm, k), lambda i, j: (i, 0))]
    args = [x]
    if mod is not None:
        blocks_per_seq = seq // tm
        in_specs.append(pl.BlockSpec((None, 6, k), lambda i, j: (i // blocks_per_seq, 0, 0)))
        args.append(mod)
    in_specs.append(pl.BlockSpec((k, tn), lambda i, j: (0, j)))
    args.append(w)
    for arr, blk, imap in extras:
        in_specs.append(pl.BlockSpec(blk, imap))
        args.append(arr)
    out_specs = [pl.BlockSpec(blk, imap) for _, _, blk, imap in outs]
    out_shape = [jax.ShapeDtypeStruct(shp, dt) for shp, dt, _, _ in outs]
    scratch = [pltpu.VMEM((tm, k), BF16)] if mod is not None else []
    kern = functools.partial(_proj_kernel, has_mod=mod is not None, n_extra=len(extras),
                             n_out=len(outs), epilogue=epilogue)
    return pl.pallas_call(
        kern,
        grid=(t // tm, n // tn),
        in_specs=in_specs,
        out_specs=out_specs,
        out_shape=out_shape,
        scratch_shapes=scratch,
        compiler_params=_cparams(("parallel", "arbitrary"), vmem_mib),
        name=name,
    )(*args)


def _rope_tables(seq, dim):
    half = dim // 2
    inv_freq = ROPE_THETA ** (-jnp.arange(half, dtype=F32) * (2.0 / dim))
    ang = jnp.arange(seq, dtype=F32)[:, None] * inv_freq[None, :]
    return jnp.cos(ang), jnp.sin(ang)


def _mla_rope_tables(seq, scale):
    cos, sin = _rope_tables(seq, MLA_ROPE_DIM)
    z32 = jnp.zeros_like(cos)
    z64 = jnp.zeros((seq, LANES - MLA_ROPE_DIM), F32)
    c = jnp.concatenate([cos, cos, z64], axis=1) * scale
    s_left = jnp.concatenate([-sin, z32, z64], axis=1) * scale
    s_right = jnp.concatenate([z32, sin, z64], axis=1) * scale
    return c, s_left, s_right


def _rope64_in_chunk(r, c, s_left, s_right):
    half = MLA_ROPE_DIM // 2
    return (r * c + pltpu.roll(r, LANES - half, 1) * s_left + pltpu.roll(r, half, 1) * s_right)


def _mla_lat_epilogue(matmul, j, extra, outs, *, q_rank, kv_rank):
    gq_ref, gkv_ref, c_ref, sl_ref, sr_ref = extra
    cq_ref, ckv_ref, kr_ref = outs
    acc = matmul(0, q_rank + kv_rank + LANES)

    def rms(v, g):
        return v * lax.rsqrt(jnp.mean(v * v, axis=-1, keepdims=True) + RMS_EPS) * g

    cq_ref[...] = rms(acc[:, :q_rank], gq_ref[...]).astype(BF16)
    ckv_ref[...] = rms(acc[:, q_rank:q_rank + kv_rank], gkv_ref[...]).astype(BF16)
    kr = acc[:, q_rank + kv_rank:]
    kr_ref[...] = _rope64_in_chunk(kr, c_ref[...], sl_ref[...], sr_ref[...]).astype(BF16)


def _mla_q_epilogue(matmul, j, extra, outs, *, heads_per_tile, scale):
    c_ref, sl_ref, sr_ref = extra
    (q_ref,) = outs
    c, sl, sr = c_ref[...], sl_ref[...], sr_ref[...]
    for h in range(heads_per_tile):
        lo = h * 2 * LANES
        acc = matmul(lo, lo + 2 * LANES)
        q_ref[:, lo:lo + LANES] = (acc[:, :LANES] * scale).astype(BF16)
        q_ref[:, lo + LANES:lo + 2 * LANES] = _rope64_in_chunk(acc[:, LANES:], c, sl, sr).astype(BF16)


def _mla_kv_kernel(ckv_ref, wkt_ref, wv_ref, kr_ref, kt_ref, v_ref, *, heads_per_tile):
    x = ckv_ref[...]
    kn_t = _dot_nt(wkt_ref[...], x)
    vv = jnp.dot(x, wv_ref[...], preferred_element_type=F32)
    kr_t = kr_ref[...].astype(F32).T.astype(BF16)
    ones = jnp.ones((x.shape[0], LANES), BF16)
    for h in range(heads_per_tile):
        lo = h * 2 * LANES
        kt_ref[lo:lo + LANES, :] = kn_t[h * LANES:(h + 1) * LANES, :].astype(BF16)
        kt_ref[lo + LANES:lo + 2 * LANES, :] = kr_t
        v_ref[:, lo:lo + LANES] = vv[:, h * LANES:(h + 1) * LANES].astype(BF16)
        v_ref[:, lo + LANES:lo + 2 * LANES] = ones


def _causal_mask(tq, tk):
    row = lax.broadcasted_iota(jnp.int32, (tq, tk), 0)
    col = lax.broadcasted_iota(jnp.int32, (tq, tk), 1)
    return col <= row


def _dot_nt(a, b):
    return lax.dot_general(a, b, (((1,), (1,)), ((), ())), preferred_element_type=F32)


def _qk(q, k_t):
    return jnp.dot(q, k_t, preferred_element_type=F32)


def _softmax_step(s, v, m_ref, l_ref, acc_ref):
    m_prev = m_ref[...]
    m_new = jnp.maximum(m_prev, jnp.max(s, axis=-1, keepdims=True))
    p = jnp.exp2(s - jnp.tile(m_new, (1, s.shape[1] // LANES)))
    alpha = jnp.exp2(m_prev - m_new)
    if l_ref is not None:
        l_ref[...] = alpha * l_ref[...] + jnp.sum(p, axis=-1, keepdims=True)
    acc_ref[...] = (acc_ref[...] * jnp.tile(alpha, (1, acc_ref.shape[1] // LANES))
                    + jnp.dot(p.astype(BF16), v, preferred_element_type=F32))
    m_ref[...] = m_new


def _mla_attn_kernel(q_ref, k_ref, v_ref, o_ref, m_ref, acc_ref, sa_ref, sb_ref, *, blk, seq, heads):
    w = 2 * LANES

    def q_body(qi, carry):
        q0 = pl.multiple_of(qi * blk, blk)
        qs = [q_ref[pl.ds(q0, blk), h * w:(h + 1) * w] for h in range(heads)]
        m_ref[...] = jnp.full(m_ref.shape, NEG_INF, F32)
        acc_ref[...] = jnp.zeros(acc_ref.shape, F32)

        def score_tile(s_ref, tile):
            k0 = pl.multiple_of(tile * blk, blk)
            keep = jnp.logical_or(_causal_mask(blk, blk), tile < qi)
            for h in range(heads):
                s = _qk(qs[h], k_ref[h * w:(h + 1) * w, pl.ds(k0, blk)])
                s_ref[h] = jnp.where(keep, s, NEG_INF)

        def consume(s_ref, tile):
            k0 = pl.multiple_of(tile * blk, blk)
            for h in range(heads):
                _softmax_step(s_ref[h], v_ref[pl.ds(k0, blk), h * w:(h + 1) * w], m_ref.at[h], None,
                              acc_ref.at[h])

        score_tile(sa_ref, 0)

        def pair_body(jj, c):
            score_tile(sb_ref, 2 * jj + 1)
            consume(sa_ref, 2 * jj)
            score_tile(sa_ref, 2 * jj + 2)
            consume(sb_ref, 2 * jj + 1)
            return c

        lax.fori_loop(0, qi // 2, pair_body, 0)

        @pl.when(qi % 2 == 1)
        def _():
            score_tile(sb_ref, qi)
            consume(sa_ref, qi - 1)
            consume(sb_ref, qi)

        @pl.when(qi % 2 == 0)
        def _():
            consume(sa_ref, qi)

        for h in range(heads):
            acc = acc_ref[h]
            o_ref[pl.ds(q0, blk), h * HEAD_DIM:(h + 1) * HEAD_DIM] = (
                acc[:, :HEAD_DIM] / acc[:, HEAD_DIM:]).astype(o_ref.dtype)
        return carry

    lax.fori_loop(0, seq // blk, q_body, 0)


def _diff_attn_kernel(q_ref, k_ref, v_ref, lam_ref, g_ref, o_ref,
                      m1_ref, l1_ref, a1_ref, m2_ref, l2_ref, a2_ref, sa_ref, sb_ref,
                      *, blk, seq, lam_init):
    hd = HEAD_DIM
    lf = lam_ref[...]
    lam_full = (jnp.exp(jnp.sum(lf[0:1, :] * lf[1:2, :], axis=-1, keepdims=True))
                - jnp.exp(jnp.sum(lf[2:3, :] * lf[3:4, :], axis=-1, keepdims=True)) + lam_init)

    def q_body(qi, carry):
        q0 = pl.multiple_of(qi * blk, blk)
        q = q_ref[pl.ds(q0, blk), :]
        q1, q2 = q[:, :hd], q[:, hd:]
        for m_ref, l_ref, a_ref in ((m1_ref, l1_ref, a1_ref), (m2_ref, l2_ref, a2_ref)):
            m_ref[...] = jnp.full(m_ref.shape, NEG_INF, F32)
            l_ref[...] = jnp.zeros(l_ref.shape, F32)
            a_ref[...] = jnp.zeros(a_ref.shape, F32)

        def score_tile(s_ref, tile):
            k0 = pl.multiple_of(tile * blk, blk)
            keep = jnp.logical_or(_causal_mask(blk, blk), tile < qi)
            s_ref[0] = jnp.where(keep, _qk(q1, k_ref[:hd, pl.ds(k0, blk)]), NEG_INF)
            s_ref[1] = jnp.where(keep, _qk(q2, k_ref[hd:, pl.ds(k0, blk)]), NEG_INF)

        def consume(s_ref, tile):
            v = v_ref[pl.ds(pl.multiple_of(tile * blk, blk), blk), :]
            _softmax_step(s_ref[0], v, m1_ref, l1_ref, a1_ref)
            _softmax_step(s_ref[1], v, m2_ref, l2_ref, a2_ref)

        score_tile(sa_ref, 0)

        def pair_body(jj, c):
            score_tile(sb_ref, 2 * jj + 1)
            consume(sa_ref, 2 * jj)
            score_tile(sa_ref, 2 * jj + 2)
            consume(sb_ref, 2 * jj + 1)
            return c

        lax.fori_loop(0, qi // 2, pair_body, 0)

        @pl.when(qi % 2 == 1)
        def _():
            score_tile(sb_ref, qi)
            consume(sa_ref, qi - 1)
            consume(sb_ref, qi)

        @pl.when(qi % 2 == 0)
        def _():
            consume(sa_ref, qi)

        reps = (1, 2 * hd // LANES)
        o = (a1_ref[...] / jnp.tile(l1_ref[...], reps)
             - lam_full * (a2_ref[...] / jnp.tile(l2_ref[...], reps)))
        o = o * lax.rsqrt(jnp.mean(o * o, axis=-1, keepdims=True) + LN_EPS) * g_ref[...]
        o_ref[pl.ds(q0, blk), :] = (o * (1.0 - lam_init)).astype(o_ref.dtype)
        return carry

    lax.fori_loop(0, seq // blk, q_body, 0)


def _sb_attn_kernel(q_ref, k_ref, v_ref, o_ref, run_ref, acc_ref, *, tq, tk, seq, heads):
    chunks_per_q = tq // tk
    hd = HEAD_DIM
    jr = lax.broadcasted_iota(jnp.int32, (2 * tk, tk), 0)
    jc = lax.broadcasted_iota(jnp.int32, (2 * tk, tk), 1)
    later_mat = (jnp.where(jr >= tk, jr - tk, jr) > jc).astype(BF16)

    def q_body(qi, carry):
        q0 = pl.multiple_of(qi * tq, tq)
        qs = [q_ref[pl.ds(q0, tq), h * hd:(h + 1) * hd] for h in range(heads)]
        run_ref[...] = jnp.zeros(run_ref.shape, F32)
        acc_ref[...] = jnp.zeros(acc_ref.shape, F32)

        def step(k0, diag_offset):
            logits = [_qk(qs[h], k_ref[h * hd:(h + 1) * hd, pl.ds(k0, tk)]) for h in range(heads)]
            for h in range(heads):
                z = logits[h]
                log_beta = jnp.minimum(z, 0.0) - jnp.log2(1.0 + jnp.exp2(-jnp.abs(z)))
                log_keep = log_beta - z
                if diag_offset is not None:
                    row = lax.broadcasted_iota(jnp.int32, (tq, tk), 0)
                    col = lax.broadcasted_iota(jnp.int32, (tq, tk), 1) + diag_offset
                    strict = col < row
                    log_keep = jnp.where(strict, log_keep, 0.0)
                hi = log_keep.astype(BF16)
                lo = (log_keep - hi.astype(F32)).astype(BF16)
                run = run_ref[h]
                later = (jnp.dot(jnp.concatenate([hi, lo], axis=1), later_mat,
                                 preferred_element_type=F32) + jnp.tile(run, (1, tk // LANES)))
                a = jnp.exp2(log_beta + later)
                if diag_offset is not None:
                    a = jnp.where(strict, a, 0.0)
                acc_ref[h] += jnp.dot(a.astype(BF16), v_ref[pl.ds(k0, tk), h * hd:(h + 1) * hd],
                                      preferred_element_type=F32)
                run_ref[h] = run + jnp.sum(log_keep, axis=-1, keepdims=True)

        for c in reversed(range(chunks_per_q)):
            step(q0 + c * tk, c * tk)

        def any_live():
            run_max = run_ref[0]
            for h in range(1, heads):
                run_max = jnp.maximum(run_max, run_ref[h])
            return jnp.max(run_max) > SB_DEAD_LOG2

        def kv_cond(carry):
            n, live = carry
            return jnp.logical_and(n < qi * chunks_per_q, live)

        def kv_body(carry):
            n, _ = carry
            kj = qi * chunks_per_q - 1 - n
            step(pl.multiple_of(kj * tk, tk), None)
            return n + 1, any_live()

        lax.while_loop(kv_cond, kv_body, (jnp.int32(0), any_live()))
        for h in range(heads):
            o_ref[pl.ds(q0, tq), h * hd:(h + 1) * hd] = acc_ref[h].astype(o_ref.dtype)
        return carry

    lax.fori_loop(0, seq // tq, q_body, 0)


def _attention(kernel, q, k, v, *, batch, seq, groups, k_off, v_off, dq, dv, d_out, scratch,
               vmem_mib, name):
    in_specs = [
        pl.BlockSpec((seq, dq), lambda b, g: (b, g)),
        pl.BlockSpec((dq, seq), lambda b, g: (k_off + g, b)),
        pl.BlockSpec((seq, dv), lambda b, g: (b, v_off + g)),
    ]
    return pl.pallas_call(
        kernel,
        grid=(batch, groups),
        in_specs=in_specs,
        out_specs=pl.BlockSpec((seq, d_out), lambda b, g: (b, g)),
        out_shape=jax.ShapeDtypeStruct((batch * seq, groups * d_out), BF16),
        scratch_shapes=scratch,
        compiler_params=_cparams(("parallel", "parallel"), vmem_mib),
        name=name,
    )(q, k, v)


def _mla_mixer(x2d, mod, seq, batch, w_in, q_norm_g, kv_norm_g, w_uq, w_ukv):
    t, d = x2d.shape
    q_rank, kv_rank = q_norm_g.shape[0], kv_norm_g.shape[0]
    heads = w_uq.shape[1] // (HEAD_DIM + MLA_ROPE_DIM)
    scale = (HEAD_DIM + MLA_ROPE_DIM) ** -0.5 * LOG2_E
    tm = min(512, seq)
    row_blocks = seq // tm

    lat_n = q_rank + kv_rank + LANES
    w_in_p = jnp.zeros((d, lat_n), BF16).at[:, :w_in.shape[1]].set(w_in.astype(BF16))
    c_k, sl_k, sr_k = _mla_rope_tables(seq, 1.0)
    c_q, sl_q, sr_q = _mla_rope_tables(seq, scale)
    tab_spec = ((tm, LANES), lambda i, j: (i % row_blocks, 0))

    cq, ckv, kr = _proj(
        x2d, w_in_p, tm=tm, tn=lat_n, mod=mod, seq=seq,
        extras=[(q_norm_g.reshape(1, q_rank), (1, q_rank), lambda i, j: (0, 0)),
                (kv_norm_g.reshape(1, kv_rank), (1, kv_rank), lambda i, j: (0, 0)),
                (c_k,) + tab_spec, (sl_k,) + tab_spec, (sr_k,) + tab_spec],
        outs=[((t, q_rank), BF16, (tm, q_rank), lambda i, j: (i, 0)),
              ((t, kv_rank), BF16, (tm, kv_rank), lambda i, j: (i, 0)),
              ((t, LANES), BF16, (tm, LANES), lambda i, j: (i, 0))],
        epilogue=functools.partial(_mla_lat_epilogue, q_rank=q_rank, kv_rank=kv_rank),
        name="mla_latent")

    w_q = w_uq.reshape(q_rank, heads, HEAD_DIM + MLA_ROPE_DIM).astype(BF16)
    w_q = jnp.pad(w_q, ((0, 0), (0, 0), (0, 2 * LANES - HEAD_DIM - MLA_ROPE_DIM)))
    w_q = w_q.reshape(q_rank, heads * 2 * LANES)
    hpt = min(heads, 8)
    (q_cat,) = _proj(
        cq, w_q, tm=tm, tn=hpt * 2 * LANES, mod=None, seq=seq,
        extras=[(c_q,) + tab_spec, (sl_q,) + tab_spec, (sr_q,) + tab_spec],
        outs=[((t, heads * 2 * LANES), BF16, (tm, hpt * 2 * LANES), lambda i, j: (i, j))],
        epilogue=functools.partial(_mla_q_epilogue, heads_per_tile=hpt, scale=scale),
        name="mla_q_up")

    w_kv = w_ukv.reshape(kv_rank, heads, 2 * HEAD_DIM).astype(BF16)
    w_kt = w_kv[:, :, :HEAD_DIM].reshape(kv_rank, heads * HEAD_DIM).T
    w_v = w_kv[:, :, HEAD_DIM:].reshape(kv_rank, heads * HEAD_DIM)
    k_cat, v = pl.pallas_call(
        functools.partial(_mla_kv_kernel, heads_per_tile=hpt),
        grid=(t // tm, heads // hpt),
        in_specs=[
            pl.BlockSpec((tm, kv_rank), lambda i, j: (i, 0)),
            pl.BlockSpec((hpt * HEAD_DIM, kv_rank), lambda i, j: (j, 0)),
            pl.BlockSpec((kv_rank, hpt * HEAD_DIM), lambda i, j: (0, j)),
            pl.BlockSpec((tm, LANES), lambda i, j: (i, 0)),
        ],
        out_specs=[
            pl.BlockSpec((hpt * 2 * LANES, tm), lambda i, j: (j, i)),
            pl.BlockSpec((tm, hpt * 2 * LANES), lambda i, j: (i, j)),
        ],
        out_shape=[jax.ShapeDtypeStruct((heads * 2 * LANES, t), BF16),
                   jax.ShapeDtypeStruct((t, heads * 2 * LANES), BF16)],
        compiler_params=_cparams(("parallel", "parallel"), 32),
        name="mla_kv_up",
    )(ckv, w_kt, w_v, kr)

    blk = min(512, seq)
    hg = ATTN_HEAD_GROUP
    return _attention(
        functools.partial(_mla_attn_kernel, blk=blk, seq=seq, heads=hg), q_cat, k_cat, v,
        batch=batch, seq=seq, groups=heads // hg, k_off=0, v_off=0,
        dq=hg * 2 * LANES, dv=hg * 2 * LANES, d_out=hg * HEAD_DIM,
        scratch=[pltpu.VMEM((hg, blk, LANES), F32), pltpu.VMEM((hg, blk, 2 * LANES), F32),
                 pltpu.VMEM((hg, blk, blk), F32), pltpu.VMEM((hg, blk, blk), F32)],
        vmem_mib=48, name="mla_attention")


def _qkv_out_specs(t, width, tm, tn):
    nq = width // tn

    def qv_map(i, j):
        return i, jnp.where(j < nq, j, jnp.where(j < 2 * nq, nq - 1, j - nq))

    def kt_map(i, j):
        return jnp.clip(j - nq, 0, nq - 1), i

    return [((t, 2 * width), BF16, (tm, tn), qv_map), ((width, t), BF16, (tn, tm), kt_map)]


def _column_parts(tn):
    part = min(tn, PROJ_PART)
    return [(c0, c0 + part) for c0 in range(0, tn, part)]


def _diff_qkv_epilogue(matmul, j, extra, outs, *, tn, width, scale):
    cos_ref, sin_ref = extra
    qv_ref, kt_ref = outs

    def rope(acc, factor):
        cos = cos_ref[...] * factor
        sin = sin_ref[...] * factor
        chunks = []
        for c in range(acc.shape[1] // LANES):
            a = acc[:, c * LANES:(c + 1) * LANES]
            chunks.append(a * cos + pltpu.roll(a, HEAD_DIM // 2, 1) * sin)
        return jnp.concatenate(chunks, axis=1)

    @pl.when(j * tn < width)
    def _():
        for c0, c1 in _column_parts(tn):
            qv_ref[:, c0:c1] = rope(matmul(c0, c1), scale).astype(BF16)

    @pl.when(jnp.logical_and(j * tn >= width, j * tn < 2 * width))
    def _():
        for c0, c1 in _column_parts(tn):
            kt_ref[c0:c1, :] = rope(matmul(c0, c1), 1.0).T.astype(BF16)

    @pl.when(j * tn >= 2 * width)
    def _():
        for c0, c1 in _column_parts(tn):
            qv_ref[:, c0:c1] = matmul(c0, c1).astype(BF16)


def _diff_mixer(x2d, mod, seq, batch, layer_idx, w_qkv, lam, subln_g):
    t, d = x2d.shape
    width = w_qkv.shape[1] // 3
    heads = width // (2 * HEAD_DIM)
    scale = HEAD_DIM ** -0.5 * LOG2_E
    lam_init = 0.8 - 0.6 * math.exp(-0.3 * layer_idx)
    tm = min(512, seq)
    tn = min(1024, width)
    row_blocks = seq // tm
    cos, sin = _rope_tables(seq, HEAD_DIM)
    cos_t = jnp.concatenate([cos, cos], axis=1)
    sin_t = jnp.concatenate([-sin, sin], axis=1)
    tab_spec = ((tm, LANES), lambda i, j: (i % row_blocks, 0))
    qv, k_t = _proj(
        x2d, w_qkv.astype(BF16), tm=tm, tn=tn, mod=mod, seq=seq,
        extras=[(cos_t,) + tab_spec, (sin_t,) + tab_spec],
        outs=_qkv_out_specs(t, width, tm, tn),
        epilogue=functools.partial(_diff_qkv_epilogue, tn=tn, width=width, scale=scale),
        name="diff_qkv")
    blk = min(512, seq)
    dh = 2 * HEAD_DIM
    in_specs = [
        pl.BlockSpec((seq, dh), lambda b, h: (b, h)),
        pl.BlockSpec((dh, seq), lambda b, h: (h, b)),
        pl.BlockSpec((seq, dh), lambda b, h: (b, heads + h)),
        pl.BlockSpec((4, HEAD_DIM), lambda b, h: (0, 0)),
        pl.BlockSpec((1, dh), lambda b, h: (0, 0)),
    ]
    stat = lambda: pltpu.VMEM((blk, LANES), F32)
    accum = lambda: pltpu.VMEM((blk, dh), F32)
    return pl.pallas_call(
        functools.partial(_diff_attn_kernel, blk=blk, seq=seq, lam_init=lam_init),
        grid=(batch, heads),
        in_specs=in_specs,
        out_specs=pl.BlockSpec((seq, dh), lambda b, h: (b, h)),
        out_shape=jax.ShapeDtypeStruct((t, width), BF16),
        scratch_shapes=[stat(), stat(), accum(), stat(), stat(), accum(),
                        pltpu.VMEM((2, blk, blk), F32), pltpu.VMEM((2, blk, blk), F32)],
        compiler_params=_cparams(("parallel", "parallel"), 40),
        name="diff_attention",
    )(qv, k_t, qv, lam, subln_g.reshape(1, dh))


def _sb_qkv_epilogue(matmul, j, extra, outs, *, tn, width, scale):
    (wkt_ref,) = extra
    qv_ref, kt_ref = outs

    @pl.when(j * tn < width)
    def _():
        for c0, c1 in _column_parts(tn):
            qv_ref[:, c0:c1] = (matmul(c0, c1) * scale).astype(BF16)

    @pl.when(jnp.logical_and(j * tn >= width, j * tn < 2 * width))
    def _():
        kt_ref[...] = matmul.transposed(wkt_ref[...]).astype(BF16)

    @pl.when(j * tn >= 2 * width)
    def _():
        for c0, c1 in _column_parts(tn):
            qv_ref[:, c0:c1] = matmul(c0, c1).astype(BF16)


def _sb_mixer(x2d, mod, seq, batch, w_qkv):
    t, d = x2d.shape
    width = w_qkv.shape[1] // 3
    heads = width // HEAD_DIM
    scale = HEAD_DIM ** -0.5 * LOG2_E
    tm = min(512, seq)
    tn = min(1024, width)
    nq = width // tn
    w_bf16 = w_qkv.astype(BF16)
    w_kt = w_bf16[:, width:2 * width].T
    qv, k_t = _proj(
        x2d, w_bf16, tm=tm, tn=tn, mod=mod, seq=seq,
        extras=[(w_kt, (tn, d), lambda i, j: (jnp.clip(j - nq, 0, nq - 1), 0))],
        outs=_qkv_out_specs(t, width, tm, tn),
        epilogue=functools.partial(_sb_qkv_epilogue, tn=tn, width=width, scale=scale),
        name="sb_qkv")
    tq = min(512, seq)
    tk = min(256, seq)
    hg = ATTN_HEAD_GROUP
    groups = heads // hg
    return _attention(
        functools.partial(_sb_attn_kernel, tq=tq, tk=tk, seq=seq, heads=hg), qv, k_t, qv,
        batch=batch, seq=seq, groups=groups, k_off=0, v_off=groups,
        dq=hg * HEAD_DIM, dv=hg * HEAD_DIM, d_out=hg * HEAD_DIM,
        scratch=[pltpu.VMEM((hg, tq, LANES), F32), pltpu.VMEM((hg, tq, HEAD_DIM), F32)],
        vmem_mib=40, name="sb_attention")


def _split_bf16(v):
    hi = v.astype(BF16)
    return hi, (v - hi.astype(F32)).astype(BF16)


def _outproj_router_kernel(o_ref, w_ref, x_ref, mod_ref, g_ref, b_ref, rw_ref, rb_ref,
                           x1_ref, idx_ref, gate_ref, *, alpha, n_experts):
    y = jnp.dot(o_ref[...], w_ref[...], preferred_element_type=F32)
    mod = mod_ref[...]
    x1 = _layer_norm_rows(alpha * x_ref[...] + mod[2:3, :] * y, g_ref[...], b_ref[...])
    x1_ref[...] = x1
    h2 = x1 * (1.0 + mod[4:5, :]) + mod[3:4, :]
    hi, lo = _split_bf16(h2)
    logits = jnp.dot(jnp.concatenate([hi, lo, hi], axis=1), rw_ref[...],
                     preferred_element_type=F32) + rb_ref[...]
    lane = lax.broadcasted_iota(jnp.int32, logits.shape, 1).astype(F32)
    work = jnp.where(lane < n_experts, logits, -jnp.inf)
    idx_out = jnp.zeros(logits.shape, F32)
    val_out = jnp.zeros(logits.shape, F32)
    top = None
    denom = None
    for kk in range(TOP_K):
        m = jnp.max(work, axis=-1, keepdims=True)
        first = jnp.min(jnp.where(work == m, lane, float(LANES)), axis=-1, keepdims=True)
        if kk == 0:
            top = m
        e = jnp.exp(m - top)
        denom = e if kk == 0 else denom + e
        idx_out = jnp.where(lane == kk, first, idx_out)
        val_out = jnp.where(lane == kk, e, val_out)
        work = jnp.where(lane == first, -jnp.inf, work)
    idx_ref[...] = idx_out.astype(jnp.int32)
    gate_ref[...] = val_out / denom


def _outproj_router(o, w_o, x2d, mod, ln_g, ln_b, router_w, router_b, *, seq, alpha):
    t, d = x2d.shape
    n_experts = router_w.shape[1]
    tm = min(256, seq)
    blocks_per_seq = seq // tm
    once = pl.Buffered(1)
    rw = jnp.zeros((d, LANES), F32).at[:, :n_experts].set(router_w)
    rw_hi, rw_lo = _split_bf16(rw)
    rw3 = jnp.concatenate([rw_hi, rw_hi, rw_lo], axis=0)
    rb = jnp.zeros((1, LANES), F32).at[0, :n_experts].set(router_b)
    row = lambda i: (i, 0)
    fixed = lambda i: (0, 0)
    return pl.pallas_call(
        functools.partial(_outproj_router_kernel, alpha=alpha, n_experts=n_experts),
        grid=(t // tm,),
        in_specs=[
            pl.BlockSpec((tm, o.shape[1]), row),
            pl.BlockSpec(w_o.shape, fixed, pipeline_mode=once),
            pl.BlockSpec((tm, d), row),
            pl.BlockSpec((None, 6, d), lambda i: (i // blocks_per_seq, 0, 0)),
            pl.BlockSpec((1, d), fixed),
            pl.BlockSpec((1, d), fixed),
            pl.BlockSpec((3 * d, LANES), fixed, pipeline_mode=once),
            pl.BlockSpec((1, LANES), fixed),
        ],
        out_specs=[pl.BlockSpec((tm, d), row),
                   pl.BlockSpec((tm, LANES), row), pl.BlockSpec((tm, LANES), row)],
        out_shape=[jax.ShapeDtypeStruct((t, d), F32),
                   jax.ShapeDtypeStruct((t, LANES), jnp.int32), jax.ShapeDtypeStruct((t, LANES), F32)],
        compiler_params=_cparams(("parallel",), 48),
        name="outproj_ln_router",
    )(o, w_o.astype(BF16), x2d, mod, ln_g.reshape(1, d), ln_b.reshape(1, d), rw3, rb)


def _plan_kernel(idx_ref, rank_ref, cnt_ref, run_ref):
    @pl.when(pl.program_id(0) == 0)
    def _():
        run_ref[...] = jnp.zeros(run_ref.shape, F32)

    idx = idx_ref[...]
    tb = idx.shape[0]
    lane = lax.broadcasted_iota(jnp.int32, idx.shape, 1)
    onehots = [(lane == idx[:, kk:kk + 1]).astype(F32) for kk in range(TOP_K)]
    chosen = onehots[0]
    for kk in range(1, TOP_K):
        chosen = chosen + onehots[kk]
    earlier = (lax.broadcasted_iota(jnp.int32, (tb, tb), 1)
               < lax.broadcasted_iota(jnp.int32, (tb, tb), 0)).astype(BF16)
    prefix = jnp.dot(earlier, chosen.astype(BF16), preferred_element_type=F32) + run_ref[0:1, :]
    rank = jnp.zeros(idx.shape, F32)
    for kk in range(TOP_K):
        rank = jnp.where(lane == kk, jnp.sum(prefix * onehots[kk], axis=-1, keepdims=True), rank)
    rank_ref[...] = rank.astype(jnp.int32)
    run_ref[...] = run_ref[...] + jnp.sum(chosen, axis=0, keepdims=True)
    cnt_ref[...] = run_ref[...]


def _routing_plan(idx_pad, n_experts, tm):
    t = idx_pad.shape[0]
    tb = min(512, t)
    rank_pad, cnt = pl.pallas_call(
        _plan_kernel,
        grid=(t // tb,),
        in_specs=[pl.BlockSpec((tb, LANES), lambda i: (i, 0))],
        out_specs=[pl.BlockSpec((tb, LANES), lambda i: (i, 0)),
                   pl.BlockSpec((SUBLANES, LANES), lambda i: (0, 0))],
        out_shape=[jax.ShapeDtypeStruct((t, LANES), jnp.int32),
                   jax.ShapeDtypeStruct((SUBLANES, LANES), F32)],
        scratch_shapes=[pltpu.VMEM((SUBLANES, LANES), F32)],
        compiler_params=_cparams(("arbitrary",), 16),
        name="moe_plan",
    )(idx_pad)
    counts = cnt[0, :n_experts].astype(jnp.int32)
    padded = (counts + tm - 1) // tm * tm
    pad_end = jnp.cumsum(padded)
    pad_start = pad_end - padded
    chose = idx_pad[:, :TOP_K, None] == jnp.arange(n_experts, dtype=jnp.int32)
    dest = (jnp.sum(jnp.where(chose, pad_start, 0), axis=-1) + rank_pad[:, :TOP_K]).astype(jnp.int32)
    n_blocks = t * TOP_K // tm + n_experts
    block_start = jnp.arange(n_blocks, dtype=jnp.int32) * tm
    block_expert = jnp.minimum(jnp.sum(block_start[:, None] >= pad_end[None, :], axis=1),
                               n_experts - 1).astype(jnp.int32)
    n_used = (pad_end[-1] // tm).astype(jnp.int32).reshape(1)
    fill_start = jnp.concatenate([pad_start + counts, pad_end[-1:]]).astype(jnp.int32)
    fill_count = jnp.concatenate([pad_end - pad_start - counts,
                                  n_blocks * tm - pad_end[-1:]]).astype(jnp.int32)
    prev_expert = jnp.concatenate([jnp.full((1,), -1, jnp.int32), block_expert[:-1]])
    run_first = (block_expert != prev_expert).astype(jnp.int32)
    ids = jnp.arange(n_experts, dtype=jnp.int32)
    later_nonempty = jnp.logical_and(ids[None, :] > ids[:, None], counts[None, :] > 0)
    next_nonempty = jnp.min(jnp.where(later_nonempty, ids[None, :], n_experts), axis=1)
    run_next = jnp.where(next_nonempty < n_experts, next_nonempty, -1).astype(jnp.int32)[block_expert]
    return dict(dest=dest.reshape(-1), block_expert=block_expert, n_used=n_used, fill_start=fill_start,
                fill_count=fill_count, n_blocks=n_blocks, run_first=run_first, run_next=run_next)


def _row_pitch(chunks):
    return chunks + 1


def _pack_pairs(v):
    w = v.shape[1] // 2
    hi = lax.bitcast_convert_type(v[:, :w].astype(BF16).astype(F32), jnp.uint32)
    lo = lax.bitcast_convert_type(v[:, w:].astype(BF16).astype(F32), jnp.uint32)
    return hi | (lo >> 16)


def _unpack_pairs(u):
    hi = lax.bitcast_convert_type(u & jnp.uint32(0xFFFF0000), F32)
    lo = lax.bitcast_convert_type(u << 16, F32)
    return jnp.concatenate([hi, lo], axis=1)


def _load_rows(ref, lead, n_rows, chunks):
    pitch = _row_pitch(chunks)
    return jnp.concatenate(
        [ref[lead + (pl.ds(c, n_rows, stride=pitch), slice(None))] for c in range(chunks)], axis=1)


def _store_rows(ref, lead, val, chunks):
    pitch = _row_pitch(chunks)
    n_rows = val.shape[0]
    for c in range(chunks):
        ref[lead + (pl.ds(c, n_rows, stride=pitch), slice(None))] = val[:, c * LANES:(c + 1) * LANES]
    ref[lead + (pl.ds(chunks, n_rows, stride=pitch), slice(None))] = jnp.zeros((n_rows, LANES), val.dtype)


def _dispatch_kernel(dest_ref, fstart_ref, fcount_ref, x_ref, mod_ref, xs_hbm, hbuf, zrow, sem, fsem,
                     *, tb, n_steps, n_fills, chunks):
    i = pl.program_id(0)
    slot = i % 2
    pitch = _row_pitch(chunks)

    def drain(buf_slot):
        for _ in range(TOP_K):
            pltpu.make_async_copy(hbuf.at[buf_slot], xs_hbm.at[pl.ds(0, tb * pitch)],
                                  sem.at[buf_slot]).wait()

    def for_each_fill(fn):
        def per_range(f, c):
            start = fstart_ref[f]

            def per_row(j, c2):
                fn(pltpu.make_async_copy(zrow, xs_hbm.at[pl.ds((start + j) * pitch, pitch)], fsem))
                return c2

            lax.fori_loop(0, fcount_ref[f], per_row, 0)
            return c

        lax.fori_loop(0, n_fills, per_range, 0)

    @pl.when(i == 0)
    def _():
        zrow[...] = jnp.zeros(zrow.shape, zrow.dtype)
        for_each_fill(lambda copy: copy.start())

    @pl.when(i >= 2)
    def _():
        drain(slot)

    mod = mod_ref[...]
    h2 = x_ref[...] * (1.0 + mod[4:5, :]) + mod[3:4, :]
    _store_rows(hbuf, (slot,), _pack_pairs(h2), chunks)

    def per_token(r, c):
        base = (i * tb + r) * TOP_K
        src = hbuf.at[slot, pl.ds(r * pitch, pitch)]
        for kk in range(TOP_K):
            pltpu.make_async_copy(src, xs_hbm.at[pl.ds(dest_ref[base + kk] * pitch, pitch)],
                                  sem.at[slot]).start()
        return c

    lax.fori_loop(0, tb, per_token, 0, unroll=4)

    @pl.when(i == n_steps - 1)
    def _():
        drain(slot)
        if n_steps > 1:
            drain(1 - slot)
        for_each_fill(lambda copy: copy.wait())


def _dispatch(x1, mod, plan, *, seq, tm):
    dest, fill_start, fill_count = plan["dest"], plan["fill_start"], plan["fill_count"]
    n_blocks = plan["n_blocks"]
    t, d = x1.shape
    chunks = d // (2 * LANES)
    pitch = _row_pitch(chunks)
    tb = min(256, seq)
    n_steps = t // tb
    blocks_per_seq = seq // tb
    grid_spec = pltpu.PrefetchScalarGridSpec(
        num_scalar_prefetch=3,
        grid=(n_steps,),
        in_specs=[
            pl.BlockSpec((tb, d), lambda i, *_: (i, 0)),
            pl.BlockSpec((None, 6, d), lambda i, *_: (i // blocks_per_seq, 0, 0)),
        ],
        out_specs=pl.BlockSpec(memory_space=pl.ANY),
        scratch_shapes=[pltpu.VMEM((2, tb * pitch, LANES), jnp.uint32),
                        pltpu.VMEM((pitch, LANES), jnp.uint32),
                        pltpu.SemaphoreType.DMA((2,)), pltpu.SemaphoreType.DMA(())],
    )
    return pl.pallas_call(
        functools.partial(_dispatch_kernel, tb=tb, n_steps=n_steps, n_fills=fill_start.shape[0],
                          chunks=chunks),
        grid_spec=grid_spec,
        out_shape=jax.ShapeDtypeStruct((n_blocks * tm * pitch, LANES), jnp.uint32),
        compiler_params=_cparams(("arbitrary",), 32),
        name="moe_dispatch",
    )(dest, fill_start, fill_count, x1, mod)


def _expert_kernel(be_ref, used_ref, first_ref, next_ref, xs_ref, wgu_hbm, bgu_ref, wd_hbm, bd_ref,
                   y_ref, gu_stage, dn_stage, gu_bf16, dn_bf16, sem, *, layer, tm, d_expert, chunks):
    i = pl.program_id(0)

    def weight_copies(expert):
        return (pltpu.make_async_copy(wgu_hbm.at[layer, expert], gu_stage, sem.at[0]),
                pltpu.make_async_copy(wd_hbm.at[layer, expert], dn_stage, sem.at[1]))

    def cast(stage, out):
        n_rows = stage.shape[0]
        slab = math.gcd(n_rows, 256)

        def body(j, c):
            rows = pl.ds(pl.multiple_of(j * slab, slab), slab)
            out[rows, :] = stage[rows, :].astype(BF16)
            return c
        lax.fori_loop(0, n_rows // slab, body, 0)

    @pl.when(i == 0)
    def _():
        for copy in weight_copies(be_ref[0]):
            copy.start()

    @pl.when(jnp.logical_and(i < used_ref[0], first_ref[i] == 1))
    def _():
        for copy in weight_copies(be_ref[i]):
            copy.wait()
        cast(gu_stage, gu_bf16)
        cast(dn_stage, dn_bf16)

        @pl.when(next_ref[i] >= 0)
        def _():
            for copy in weight_copies(next_ref[i]):
                copy.start()

    @pl.when(i < used_ref[0])
    def _():
        x = _unpack_pairs(_load_rows(xs_ref, (), tm, chunks)).astype(BF16)
        gu = jnp.dot(x, gu_bf16[...], preferred_element_type=F32) + bgu_ref[...]
        g = jnp.minimum(gu[:, :d_expert], SWIGLU_LIMIT)
        lin = jnp.clip(gu[:, d_expert:], -SWIGLU_LIMIT, SWIGLU_LIMIT)
        act = g * jax.nn.sigmoid(SWIGLU_ALPHA * g) * (lin + 1.0)
        y = jnp.dot(act.astype(BF16), dn_bf16[...], preferred_element_type=F32) + bd_ref[...]
        _store_rows(y_ref, (), _pack_pairs(y), chunks)

    @pl.when(i >= used_ref[0])
    def _():
        y_ref[...] = jnp.zeros(y_ref.shape, y_ref.dtype)


def _experts(xs, plan, layer, w_gate_up, b_gate_up, w_down, b_down, tm):
    n_blocks = plan["n_blocks"]
    depth, n_experts, d_expert, d = w_down.shape
    chunks = d // (2 * LANES)
    pitch = _row_pitch(chunks)
    per_expert = lambda i, be, used, first, nxt: (layer, be[i], 0, 0)
    rows = lambda i, be, used, first, nxt: (i, 0)
    used_rows = lambda i, be, used, first, nxt: (jnp.minimum(i, used[0] - 1), 0)
    grid_spec = pltpu.PrefetchScalarGridSpec(
        num_scalar_prefetch=4,
        grid=(n_blocks,),
        in_specs=[
            pl.BlockSpec((tm * pitch, LANES), used_rows),
            pl.BlockSpec(memory_space=pl.ANY),
            pl.BlockSpec((None, None, 1, 2 * d_expert), per_expert),
            pl.BlockSpec(memory_space=pl.ANY),
            pl.BlockSpec((None, None, 1, d), per_expert),
        ],
        out_specs=pl.BlockSpec((tm * pitch, LANES), rows),
        scratch_shapes=[pltpu.VMEM((d, 2 * d_expert), F32), pltpu.VMEM((d_expert, d), F32),
                        pltpu.VMEM((d, 2 * d_expert), BF16), pltpu.VMEM((d_expert, d), BF16),
                        pltpu.SemaphoreType.DMA((2,))],
    )
    return pl.pallas_call(
        functools.partial(_expert_kernel, layer=layer, tm=tm, d_expert=d_expert, chunks=chunks),
        grid_spec=grid_spec,
        out_shape=jax.ShapeDtypeStruct((n_blocks * tm * pitch, LANES), jnp.uint32),
        compiler_params=_cparams(("arbitrary",), 56),
        name="moe_experts",
    )(plan["block_expert"], plan["n_used"], plan["run_first"], plan["run_next"], xs, w_gate_up,
      b_gate_up.reshape(depth, n_experts, 1, 2 * d_expert), w_down,
      b_down.reshape(depth, n_experts, 1, d))


def _combine_kernel(dest_ref, ys_hbm, x_ref, gate_ref, mod_ref, g_ref, b_ref, o_ref, ybuf, sem,
                    *, tn, n_blocks, alpha, chunks):
    i = pl.program_id(0)
    slot = i % 2
    pitch = _row_pitch(chunks)

    def issue(block, buf_slot):
        def body(r, c):
            base = (block * tn + r) * TOP_K
            for kk in range(TOP_K):
                pltpu.make_async_copy(ys_hbm.at[pl.ds(dest_ref[base + kk] * pitch, pitch)],
                                      ybuf.at[buf_slot, kk, pl.ds(r * pitch, pitch)],
                                      sem.at[buf_slot]).start()
            return c
        lax.fori_loop(0, tn, body, 0, unroll=4)

    @pl.when(i == 0)
    def _():
        issue(0, 0)

    @pl.when(i + 1 < n_blocks)
    def _():
        issue(i + 1, 1 - slot)

    for kk in range(TOP_K):
        pltpu.make_async_copy(ys_hbm.at[pl.ds(0, tn * pitch)], ybuf.at[slot, kk], sem.at[slot]).wait()

    gate = gate_ref[...]
    y = None
    for kk in range(TOP_K):
        term = gate[:, kk:kk + 1] * _unpack_pairs(_load_rows(ybuf, (slot, kk), tn, chunks))
        y = term if y is None else y + term
    mod = mod_ref[...]
    o_ref[...] = _layer_norm_rows(alpha * x_ref[...] + mod[5:6, :] * y, g_ref[...], b_ref[...])


def _combine(ys, dest, x1, gate_pad, mod, ln_g, ln_b, *, seq, alpha):
    t, d = x1.shape
    chunks = d // (2 * LANES)
    pitch = _row_pitch(chunks)
    tn = min(128, seq)
    n_blocks = t // tn
    blocks_per_seq = seq // tn
    grid_spec = pltpu.PrefetchScalarGridSpec(
        num_scalar_prefetch=1,
        grid=(n_blocks,),
        in_specs=[
            pl.BlockSpec(memory_space=pl.ANY),
            pl.BlockSpec((tn, d), lambda i, dst: (i, 0)),
            pl.BlockSpec((tn, LANES), lambda i, dst: (i, 0)),
            pl.BlockSpec((None, 6, d), lambda i, dst: (i // blocks_per_seq, 0, 0)),
            pl.BlockSpec((1, d), lambda i, dst: (0, 0)),
            pl.BlockSpec((1, d), lambda i, dst: (0, 0)),
        ],
        out_specs=pl.BlockSpec((tn, d), lambda i, dst: (i, 0)),
        scratch_shapes=[pltpu.VMEM((2, TOP_K, tn * pitch, LANES), jnp.uint32), pltpu.SemaphoreType.DMA((2,))],
    )
    return pl.pallas_call(
        functools.partial(_combine_kernel, tn=tn, n_blocks=n_blocks, alpha=alpha, chunks=chunks),
        grid_spec=grid_spec,
        out_shape=jax.ShapeDtypeStruct((t, d), F32),
        compiler_params=_cparams(("arbitrary",), 40),
        name="moe_combine_ln",
    )(dest, ys, x1, gate_pad, mod, ln_g.reshape(1, d), ln_b.reshape(1, d))


def kernel(x, c, ada_w, ada_b, ln_g, ln_b, mla_w_in, mla_q_norm_g, mla_kv_norm_g, mla_w_uq, mla_w_ukv,
           mla_w_o, diff_w_qkv, diff_lambda, diff_subln_g, diff_w_o, sb_w_qkv, sb_w_o, moe_router_w,
           moe_router_b, moe_w_gate_up, moe_b_gate_up, moe_w_down, moe_b_down):
    batch, seq, d = x.shape
    depth = ada_w.shape[0]
    alpha = (2 * depth) ** 0.25
    n_experts = moe_router_w.shape[-1]
    moe_tm = 256
    mods = _adaln(c, ada_w, ada_b)
    x2d = x.reshape(batch * seq, d)
    for i in range(depth):
        mod = mods[i]
        kind, j = i % N_MIXERS, i // N_MIXERS
        if kind == 0:
            o = _mla_mixer(x2d, mod, seq, batch, mla_w_in[j], mla_q_norm_g[j], mla_kv_norm_g[j],
                           mla_w_uq[j], mla_w_ukv[j])
            w_o = mla_w_o[j]
        elif kind == 1:
            o = _diff_mixer(x2d, mod, seq, batch, i, diff_w_qkv[j], diff_lambda[j], diff_subln_g[j])
            w_o = diff_w_o[j]
        else:
            o = _sb_mixer(x2d, mod, seq, batch, sb_w_qkv[j])
            w_o = sb_w_o[j]
        x1, idx_pad, gate_pad = _outproj_router(
            o, w_o, x2d, mod, ln_g[i, 0], ln_b[i, 0], moe_router_w[i], moe_router_b[i],
            seq=seq, alpha=alpha)
        plan = _routing_plan(idx_pad, n_experts, moe_tm)
        xs = _dispatch(x1, mod, plan, seq=seq, tm=moe_tm)
        ys = _experts(xs, plan, i, moe_w_gate_up, moe_b_gate_up, moe_w_down, moe_b_down, moe_tm)
        x2d = _combine(ys, plan["dest"], x1, gate_pad, mod, ln_g[i, 1], ln_b[i, 1], seq=seq,
                       alpha=alpha)
    return x2d.reshape(batch, seq, d)
```

```python
import functools
import math

import jax
import jax.numpy as jnp
from jax import lax
from jax.experimental import pallas as pl
from jax.experimental.pallas import tpu as pltpu

F32 = jnp.float32
BF16 = jnp.bfloat16

N_MIXERS = 3
NEG_INF = -1e30
ROPE_THETA = 10000.0
LN_EPS = 1e-5
RMS_EPS = 1e-6
HEAD_DIM = 128
MLA_ROPE_DIM = 64
TOP_K = 4
SWIGLU_LIMIT = 7.0
SWIGLU_ALPHA = 1.702
LANES = 128
SUBLANES = 8
LOG2_E = 1.4426950408889634
ATTN_HEAD_GROUP = 2
PROJ_PART = 512
SB_DEAD_LOG2 = -160.0
MIB = 1024 * 1024


def _cparams(sems, vmem_mib):
    return pltpu.CompilerParams(dimension_semantics=sems, vmem_limit_bytes=vmem_mib * MIB)


def _layer_norm_rows(r, g, b):
    mu = jnp.mean(r, axis=-1, keepdims=True)
    d = r - mu
    var = jnp.mean(d * d, axis=-1, keepdims=True)
    return d * lax.rsqrt(var + LN_EPS) * g + b


def _adaln_kernel(c_ref, w_ref, b_ref, o_ref):
    c = c_ref[...]
    cond = c * jax.nn.sigmoid(c)
    o_ref[...] = jnp.dot(cond.astype(BF16), w_ref[...].astype(BF16),
                         preferred_element_type=F32) + b_ref[...]


def _adaln(c, ada_w, ada_b):
    depth, d, n = ada_w.shape
    b = c.shape[0]
    rows = 16
    tn = next(cand for cand in (1024, 512, 256, 128) if n % cand == 0)
    c_pad = jnp.zeros((rows, d), F32).at[:b].set(c)
    out = pl.pallas_call(
        _adaln_kernel,
        grid=(depth, n // tn),
        in_specs=[
            pl.BlockSpec((rows, d), lambda l, j: (0, 0)),
            pl.BlockSpec((None, d, tn), lambda l, j: (l, 0, j)),
            pl.BlockSpec((None, 1, tn), lambda l, j: (l, 0, j)),
        ],
        out_specs=pl.BlockSpec((None, rows, tn), lambda l, j: (l, 0, j)),
        out_shape=jax.ShapeDtypeStruct((depth, rows, n), F32),
        compiler_params=_cparams(("parallel", "parallel"), 40),
        name="adaln",
    )(c_pad, ada_w, ada_b.reshape(depth, 1, n))
    return out[:, :b].reshape(depth, b, 6, d)


def _proj_kernel(*refs, has_mod, n_extra, n_out, epilogue):
    x_ref = refs[0]
    pos = 1
    mod_ref = None
    if has_mod:
        mod_ref = refs[pos]
        pos += 1
    w_ref = refs[pos]
    pos += 1
    extra = refs[pos:pos + n_extra]
    pos += n_extra
    outs = refs[pos:pos + n_out]
    pos += n_out
    if has_mod:
        h_ref = refs[pos]

        @pl.when(pl.program_id(1) == 0)
        def _():
            mod = mod_ref[...]
            h_ref[...] = (x_ref[...] * (1.0 + mod[1:2, :]) + mod[0:1, :]).astype(BF16)

        h = h_ref[...]
    else:
        h = x_ref[...]
    def matmul(c0, c1):
        return jnp.dot(h, w_ref[:, c0:c1], preferred_element_type=F32)

    matmul.transposed = lambda w_t: _dot_nt(w_t, h)
    epilogue(matmul, pl.program_id(1), extra, outs)


def _proj(x, w, *, tm, tn, mod, seq, extras, outs, epilogue, vmem_mib=48, name):
    t, k = x.shape
    n = w.shape[1]
    in_specs = [pl.BlockSpec((tm, k), lambda i, j: (i, 0))]
    args = [x]
    if mod is not None:
        blocks_per_seq = seq // tm
        in_specs.append(pl.BlockSpec((None, 6, k), lambda i, j: (i // blocks_per_seq, 0, 0)))
        args.append(mod)
    in_specs.append(pl.BlockSpec((k, tn), lambda i, j: (0, j)))
    args.append(w)
    for arr, blk, imap in extras:
        in_specs.append(pl.BlockSpec(blk, imap))
        args.append(arr)
    out_specs = [pl.BlockSpec(blk, imap) for _, _, blk, imap in outs]
    out_shape = [jax.ShapeDtypeStruct(shp, dt) for shp, dt, _, _ in outs]
    scratch = [pltpu.VMEM((tm, k), BF16)] if mod is not None else []
    kern = functools.partial(_proj_kernel, has_mod=mod is not None, n_extra=len(extras),
                             n_out=len(outs), epilogue=epilogue)
    return pl.pallas_call(
        kern,
        grid=(t // tm, n // tn),
        in_specs=in_specs,
        out_specs=out_specs,
        out_shape=out_shape,
        scratch_shapes=scratch,
        compiler_params=_cparams(("parallel", "arbitrary"), vmem_mib),
        name=name,
    )(*args)


def _rope_tables(seq, dim):
    half = dim // 2
    inv_freq = ROPE_THETA ** (-jnp.arange(half, dtype=F32) * (2.0 / dim))
    ang = jnp.arange(seq, dtype=F32)[:, None] * inv_freq[None, :]
    return jnp.cos(ang), jnp.sin(ang)


def _mla_rope_tables(seq, scale):
    cos, sin = _rope_tables(seq, MLA_ROPE_DIM)
    z32 = jnp.zeros_like(cos)
    z64 = jnp.zeros((seq, LANES - MLA_ROPE_DIM), F32)
    c = jnp.concatenate([cos, cos, z64], axis=1) * scale
    s_left = jnp.concatenate([-sin, z32, z64], axis=1) * scale
    s_right = jnp.concatenate([z32, sin, z64], axis=1) * scale
    return c, s_left, s_right


def _rope64_in_chunk(r, c, s_left, s_right):
    half = MLA_ROPE_DIM // 2
    return (r * c + pltpu.roll(r, LANES - half, 1) * s_left + pltpu.roll(r, half, 1) * s_right)


def _mla_lat_epilogue(matmul, j, extra, outs, *, q_rank, kv_rank):
    gq_ref, gkv_ref, c_ref, sl_ref, sr_ref = extra
    cq_ref, ckv_ref, kr_ref = outs
    acc = matmul(0, q_rank + kv_rank + LANES)

    def rms(v, g):
        return v * lax.rsqrt(jnp.mean(v * v, axis=-1, keepdims=True) + RMS_EPS) * g

    cq_ref[...] = rms(acc[:, :q_rank], gq_ref[...]).astype(BF16)
    ckv_ref[...] = rms(acc[:, q_rank:q_rank + kv_rank], gkv_ref[...]).astype(BF16)
    kr = acc[:, q_rank + kv_rank:]
    kr_ref[...] = _rope64_in_chunk(kr, c_ref[...], sl_ref[...], sr_ref[...]).astype(BF16)


def _mla_q_epilogue(matmul, j, extra, outs, *, heads_per_tile, scale):
    c_ref, sl_ref, sr_ref = extra
    (q_ref,) = outs
    c, sl, sr = c_ref[...], sl_ref[...], sr_ref[...]
    for h in range(heads_per_tile):
        lo = h * 2 * LANES
        acc = matmul(lo, lo + 2 * LANES)
        q_ref[:, lo:lo + LANES] = (acc[:, :LANES] * scale).astype(BF16)
        q_ref[:, lo + LANES:lo + 2 * LANES] = _rope64_in_chunk(acc[:, LANES:], c, sl, sr).astype(BF16)


def _mla_kv_kernel(ckv_ref, wkt_ref, wv_ref, kr_ref, kt_ref, v_ref, *, heads_per_tile):
    x = ckv_ref[...]
    kn_t = _dot_nt(wkt_ref[...], x)
    vv = jnp.dot(x, wv_ref[...], preferred_element_type=F32)
    kr_t = kr_ref[...].astype(F32).T.astype(BF16)
    ones = jnp.ones((x.shape[0], LANES), BF16)
    for h in range(heads_per_tile):
        lo = h * 2 * LANES
        kt_ref[lo:lo + LANES, :] = kn_t[h * LANES:(h + 1) * LANES, :].astype(BF16)
        kt_ref[lo + LANES:lo + 2 * LANES, :] = kr_t
        v_ref[:, lo:lo + LANES] = vv[:, h * LANES:(h + 1) * LANES].astype(BF16)
        v_ref[:, lo + LANES:lo + 2 * LANES] = ones


def _causal_mask(tq, tk):
    row = lax.broadcasted_iota(jnp.int32, (tq, tk), 0)
    col = lax.broadcasted_iota(jnp.int32, (tq, tk), 1)
    return col <= row


def _dot_nt(a, b):
    return lax.dot_general(a, b, (((1,), (1,)), ((), ())), preferred_element_type=F32)


def _qk(q, k_t):
    return jnp.dot(q, k_t, preferred_element_type=F32)


def _softmax_step(s, v, m_ref, l_ref, acc_ref):
    m_prev = m_ref[...]
    m_new = jnp.maximum(m_prev, jnp.max(s, axis=-1, keepdims=True))
    p = jnp.exp2(s - jnp.tile(m_new, (1, s.shape[1] // LANES)))
    alpha = jnp.exp2(m_prev - m_new)
    if l_ref is not None:
        l_ref[...] = alpha * l_ref[...] + jnp.sum(p, axis=-1, keepdims=True)
    acc_ref[...] = (acc_ref[...] * jnp.tile(alpha, (1, acc_ref.shape[1] // LANES))
                    + jnp.dot(p.astype(BF16), v, preferred_element_type=F32))
    m_ref[...] = m_new


def _mla_attn_kernel(q_ref, k_ref, v_ref, o_ref, m_ref, acc_ref, sa_ref, sb_ref, *, blk, seq, heads):
    w = 2 * LANES

    def q_body(qi, carry):
        q0 = pl.multiple_of(qi * blk, blk)
        qs = [q_ref[pl.ds(q0, blk), h * w:(h + 1) * w] for h in range(heads)]
        m_ref[...] = jnp.full(m_ref.shape, NEG_INF, F32)
        acc_ref[...] = jnp.zeros(acc_ref.shape, F32)

        def score_tile(s_ref, tile, below_diagonal=False):
            k0 = pl.multiple_of(tile * blk, blk)
            keep = None if below_diagonal else jnp.logical_or(_causal_mask(blk, blk), tile < qi)
            for h in range(heads):
                s = _qk(qs[h], k_ref[h * w:(h + 1) * w, pl.ds(k0, blk)])
                s_ref[h] = s if below_diagonal else jnp.where(keep, s, NEG_INF)

        def consume(s_ref, tile):
            k0 = pl.multiple_of(tile * blk, blk)
            for h in range(heads):
                _softmax_step(s_ref[h], v_ref[pl.ds(k0, blk), h * w:(h + 1) * w], m_ref.at[h], None,
                              acc_ref.at[h])

        score_tile(sa_ref, 0)

        def pair_body(jj, c):
            score_tile(sb_ref, 2 * jj + 1, below_diagonal=True)
            consume(sa_ref, 2 * jj)
            score_tile(sa_ref, 2 * jj + 2)
            consume(sb_ref, 2 * jj + 1)
            return c

        lax.fori_loop(0, qi // 2, pair_body, 0)

        @pl.when(qi % 2 == 1)
        def _():
            score_tile(sb_ref, qi)
            consume(sa_ref, qi - 1)
            consume(sb_ref, qi)

        @pl.when(qi % 2 == 0)
        def _():
            consume(sa_ref, qi)

        for h in range(heads):
            acc = acc_ref[h]
            o_ref[pl.ds(q0, blk), h * HEAD_DIM:(h + 1) * HEAD_DIM] = (
                acc[:, :HEAD_DIM] / acc[:, HEAD_DIM:]).astype(o_ref.dtype)
        return carry

    lax.fori_loop(0, seq // blk, q_body, 0)


def _diff_attn_kernel(q_ref, k_ref, v_ref, lam_ref, g_ref, o_ref,
                      m1_ref, l1_ref, a1_ref, m2_ref, l2_ref, a2_ref, sa_ref, sb_ref,
                      *, blk, seq, lam_init):
    hd = HEAD_DIM
    lf = lam_ref[...]
    lam_full = (jnp.exp(jnp.sum(lf[0:1, :] * lf[1:2, :], axis=-1, keepdims=True))
                - jnp.exp(jnp.sum(lf[2:3, :] * lf[3:4, :], axis=-1, keepdims=True)) + lam_init)

    def q_body(qi, carry):
        q0 = pl.multiple_of(qi * blk, blk)
        q = q_ref[pl.ds(q0, blk), :]
        q1, q2 = q[:, :hd], q[:, hd:]
        for m_ref, l_ref, a_ref in ((m1_ref, l1_ref, a1_ref), (m2_ref, l2_ref, a2_ref)):
            m_ref[...] = jnp.full(m_ref.shape, NEG_INF, F32)
            l_ref[...] = jnp.zeros(l_ref.shape, F32)
            a_ref[...] = jnp.zeros(a_ref.shape, F32)

        def score_tile(s_ref, tile, below_diagonal=False):
            k0 = pl.multiple_of(tile * blk, blk)
            s1 = _qk(q1, k_ref[:hd, pl.ds(k0, blk)])
            s2 = _qk(q2, k_ref[hd:, pl.ds(k0, blk)])
            if not below_diagonal:
                keep = jnp.logical_or(_causal_mask(blk, blk), tile < qi)
                s1 = jnp.where(keep, s1, NEG_INF)
                s2 = jnp.where(keep, s2, NEG_INF)
            s_ref[0] = s1
            s_ref[1] = s2

        def consume(s_ref, tile):
            v = v_ref[pl.ds(pl.multiple_of(tile * blk, blk), blk), :]
            _softmax_step(s_ref[0], v, m1_ref, l1_ref, a1_ref)
            _softmax_step(s_ref[1], v, m2_ref, l2_ref, a2_ref)

        score_tile(sa_ref, 0)

        def pair_body(jj, c):
            score_tile(sb_ref, 2 * jj + 1, below_diagonal=True)
            consume(sa_ref, 2 * jj)
            score_tile(sa_ref, 2 * jj + 2)
            consume(sb_ref, 2 * jj + 1)
            return c

        lax.fori_loop(0, qi // 2, pair_body, 0)

        @pl.when(qi % 2 == 1)
        def _():
            score_tile(sb_ref, qi)
            consume(sa_ref, qi - 1)
            consume(sb_ref, qi)

        @pl.when(qi % 2 == 0)
        def _():
            consume(sa_ref, qi)

        reps = (1, 2 * hd // LANES)
        o = (a1_ref[...] / jnp.tile(l1_ref[...], reps)
             - lam_full * (a2_ref[...] / jnp.tile(l2_ref[...], reps)))
        o = o * lax.rsqrt(jnp.mean(o * o, axis=-1, keepdims=True) + LN_EPS) * g_ref[...]
        o_ref[pl.ds(q0, blk), :] = (o * (1.0 - lam_init)).astype(o_ref.dtype)
        return carry

    lax.fori_loop(0, seq // blk, q_body, 0)


def _sb_attn_kernel(q_ref, k_ref, v_ref, o_ref, run_ref, acc_ref, *, tq, tk, seq, heads):
    chunks_per_q = tq // tk
    hd = HEAD_DIM
    jr = lax.broadcasted_iota(jnp.int32, (2 * tk, tk), 0)
    jc = lax.broadcasted_iota(jnp.int32, (2 * tk, tk), 1)
    later_mat = (jnp.where(jr >= tk, jr - tk, jr) > jc).astype(BF16)

    def q_body(qi, carry):
        q0 = pl.multiple_of(qi * tq, tq)
        qs = [q_ref[pl.ds(q0, tq), h * hd:(h + 1) * hd] for h in range(heads)]
        run_ref[...] = jnp.zeros(run_ref.shape, F32)
        acc_ref[...] = jnp.zeros(acc_ref.shape, F32)

        def step(k0, diag_offset):
            logits = [_qk(qs[h], k_ref[h * hd:(h + 1) * hd, pl.ds(k0, tk)]) for h in range(heads)]
            for h in range(heads):
                z = logits[h]
                log_beta = jnp.minimum(z, 0.0) - jnp.log2(1.0 + jnp.exp2(-jnp.abs(z)))
                log_keep = log_beta - z
                if diag_offset is not None:
                    row = lax.broadcasted_iota(jnp.int32, (tq, tk), 0)
                    col = lax.broadcasted_iota(jnp.int32, (tq, tk), 1) + diag_offset
                    strict = col < row
                    log_keep = jnp.where(strict, log_keep, 0.0)
                hi = log_keep.astype(BF16)
                lo = (log_keep - hi.astype(F32)).astype(BF16)
                run = run_ref[h]
                later = (jnp.dot(jnp.concatenate([hi, lo], axis=1), later_mat,
                                 preferred_element_type=F32) + jnp.tile(run, (1, tk // LANES)))
                a = jnp.exp2(log_beta + later)
                if diag_offset is not None:
                    a = jnp.where(strict, a, 0.0)
                acc_ref[h] += jnp.dot(a.astype(BF16), v_ref[pl.ds(k0, tk), h * hd:(h + 1) * hd],
                                      preferred_element_type=F32)
                run_ref[h] = run + jnp.sum(log_keep, axis=-1, keepdims=True)

        for c in reversed(range(chunks_per_q)):
            step(q0 + c * tk, c * tk)

        def any_live():
            run_max = run_ref[0]
            for h in range(1, heads):
                run_max = jnp.maximum(run_max, run_ref[h])
            return jnp.max(run_max) > SB_DEAD_LOG2

        def kv_cond(carry):
            n, live = carry
            return jnp.logical_and(n < qi * chunks_per_q, live)

        def kv_body(carry):
            n, _ = carry
            kj = qi * chunks_per_q - 1 - n
            step(pl.multiple_of(kj * tk, tk), None)
            return n + 1, any_live()

        lax.while_loop(kv_cond, kv_body, (jnp.int32(0), any_live()))
        for h in range(heads):
            o_ref[pl.ds(q0, tq), h * hd:(h + 1) * hd] = acc_ref[h].astype(o_ref.dtype)
        return carry

    lax.fori_loop(0, seq // tq, q_body, 0)


def _attention(kernel, q, k, v, *, batch, seq, groups, k_off, v_off, dq, dv, d_out, scratch,
               vmem_mib, name):
    in_specs = [
        pl.BlockSpec((seq, dq), lambda b, g: (b, g)),
        pl.BlockSpec((dq, seq), lambda b, g: (k_off + g, b)),
        pl.BlockSpec((seq, dv), lambda b, g: (b, v_off + g)),
    ]
    return pl.pallas_call(
        kernel,
        grid=(batch, groups),
        in_specs=in_specs,
        out_specs=pl.BlockSpec((seq, d_out), lambda b, g: (b, g)),
        out_shape=jax.ShapeDtypeStruct((batch * seq, groups * d_out), BF16),
        scratch_shapes=scratch,
        compiler_params=_cparams(("parallel", "parallel"), vmem_mib),
        name=name,
    )(q, k, v)


def _mla_mixer(x2d, mod, seq, batch, w_in, q_norm_g, kv_norm_g, w_uq, w_ukv):
    t, d = x2d.shape
    q_rank, kv_rank = q_norm_g.shape[0], kv_norm_g.shape[0]
    heads = w_uq.shape[1] // (HEAD_DIM + MLA_ROPE_DIM)
    scale = (HEAD_DIM + MLA_ROPE_DIM) ** -0.5 * LOG2_E
    tm = min(512, seq)
    row_blocks = seq // tm

    lat_n = q_rank + kv_rank + LANES
    w_in_p = jnp.zeros((d, lat_n), BF16).at[:, :w_in.shape[1]].set(w_in.astype(BF16))
    c_k, sl_k, sr_k = _mla_rope_tables(seq, 1.0)
    c_q, sl_q, sr_q = _mla_rope_tables(seq, scale)
    tab_spec = ((tm, LANES), lambda i, j: (i % row_blocks, 0))

    cq, ckv, kr = _proj(
        x2d, w_in_p, tm=tm, tn=lat_n, mod=mod, seq=seq,
        extras=[(q_norm_g.reshape(1, q_rank), (1, q_rank), lambda i, j: (0, 0)),
                (kv_norm_g.reshape(1, kv_rank), (1, kv_rank), lambda i, j: (0, 0)),
                (c_k,) + tab_spec, (sl_k,) + tab_spec, (sr_k,) + tab_spec],
        outs=[((t, q_rank), BF16, (tm, q_rank), lambda i, j: (i, 0)),
              ((t, kv_rank), BF16, (tm, kv_rank), lambda i, j: (i, 0)),
              ((t, LANES), BF16, (tm, LANES), lambda i, j: (i, 0))],
        epilogue=functools.partial(_mla_lat_epilogue, q_rank=q_rank, kv_rank=kv_rank),
        name="mla_latent")

    w_q = w_uq.reshape(q_rank, heads, HEAD_DIM + MLA_ROPE_DIM).astype(BF16)
    w_q = jnp.pad(w_q, ((0, 0), (0, 0), (0, 2 * LANES - HEAD_DIM - MLA_ROPE_DIM)))
    w_q = w_q.reshape(q_rank, heads * 2 * LANES)
    hpt = min(heads, 8)
    (q_cat,) = _proj(
        cq, w_q, tm=tm, tn=hpt * 2 * LANES, mod=None, seq=seq,
        extras=[(c_q,) + tab_spec, (sl_q,) + tab_spec, (sr_q,) + tab_spec],
        outs=[((t, heads * 2 * LANES), BF16, (tm, hpt * 2 * LANES), lambda i, j: (i, j))],
        epilogue=functools.partial(_mla_q_epilogue, heads_per_tile=hpt, scale=scale),
        name="mla_q_up")

    w_kv = w_ukv.reshape(kv_rank, heads, 2 * HEAD_DIM).astype(BF16)
    w_kt = w_kv[:, :, :HEAD_DIM].reshape(kv_rank, heads * HEAD_DIM).T
    w_v = w_kv[:, :, HEAD_DIM:].reshape(kv_rank, heads * HEAD_DIM)
    k_cat, v = pl.pallas_call(
        functools.partial(_mla_kv_kernel, heads_per_tile=hpt),
        grid=(t // tm, heads // hpt),
        in_specs=[
            pl.BlockSpec((tm, kv_rank), lambda i, j: (i, 0)),
            pl.BlockSpec((hpt * HEAD_DIM, kv_rank), lambda i, j: (j, 0)),
            pl.BlockSpec((kv_rank, hpt * HEAD_DIM), lambda i, j: (0, j)),
            pl.BlockSpec((tm, LANES), lambda i, j: (i, 0)),
        ],
        out_specs=[
            pl.BlockSpec((hpt * 2 * LANES, tm), lambda i, j: (j, i)),
            pl.BlockSpec((tm, hpt * 2 * LANES), lambda i, j: (i, j)),
        ],
        out_shape=[jax.ShapeDtypeStruct((heads * 2 * LANES, t), BF16),
                   jax.ShapeDtypeStruct((t, heads * 2 * LANES), BF16)],
        compiler_params=_cparams(("parallel", "parallel"), 32),
        name="mla_kv_up",
    )(ckv, w_kt, w_v, kr)

    blk = min(512, seq)
    hg = ATTN_HEAD_GROUP
    return _attention(
        functools.partial(_mla_attn_kernel, blk=blk, seq=seq, heads=hg), q_cat, k_cat, v,
        batch=batch, seq=seq, groups=heads // hg, k_off=0, v_off=0,
        dq=hg * 2 * LANES, dv=hg * 2 * LANES, d_out=hg * HEAD_DIM,
        scratch=[pltpu.VMEM((hg, blk, LANES), F32), pltpu.VMEM((hg, blk, 2 * LANES), F32),
                 pltpu.VMEM((hg, blk, blk), F32), pltpu.VMEM((hg, blk, blk), F32)],
        vmem_mib=48, name="mla_attention")


def _qkv_out_specs(t, width, tm, tn):
    nq = width // tn

    def qv_map(i, j):
        return i, jnp.where(j < nq, j, jnp.where(j < 2 * nq, nq - 1, j - nq))

    def kt_map(i, j):
        return jnp.clip(j - nq, 0, nq - 1), i

    return [((t, 2 * width), BF16, (tm, tn), qv_map), ((width, t), BF16, (tn, tm), kt_map)]


def _column_parts(tn):
    part = min(tn, PROJ_PART)
    return [(c0, c0 + part) for c0 in range(0, tn, part)]


def _diff_qkv_epilogue(matmul, j, extra, outs, *, tn, width, scale):
    cos_ref, sin_ref = extra
    qv_ref, kt_ref = outs

    def rope(acc, factor):
        cos = cos_ref[...] * factor
        sin = sin_ref[...] * factor
        chunks = []
        for c in range(acc.shape[1] // LANES):
            a = acc[:, c * LANES:(c + 1) * LANES]
            chunks.append(a * cos + pltpu.roll(a, HEAD_DIM // 2, 1) * sin)
        return jnp.concatenate(chunks, axis=1)

    @pl.when(j * tn < width)
    def _():
        for c0, c1 in _column_parts(tn):
            qv_ref[:, c0:c1] = rope(matmul(c0, c1), scale).astype(BF16)

    @pl.when(jnp.logical_and(j * tn >= width, j * tn < 2 * width))
    def _():
        for c0, c1 in _column_parts(tn):
            kt_ref[c0:c1, :] = rope(matmul(c0, c1), 1.0).T.astype(BF16)

    @pl.when(j * tn >= 2 * width)
    def _():
        for c0, c1 in _column_parts(tn):
            qv_ref[:, c0:c1] = matmul(c0, c1).astype(BF16)


def _diff_mixer(x2d, mod, seq, batch, layer_idx, w_qkv, lam, subln_g):
    t, d = x2d.shape
    width = w_qkv.shape[1] // 3
    heads = width // (2 * HEAD_DIM)
    scale = HEAD_DIM ** -0.5 * LOG2_E
    lam_init = 0.8 - 0.6 * math.exp(-0.3 * layer_idx)
    tm = min(512, seq)
    tn = min(1024, width)
    row_blocks = seq // tm
    cos, sin = _rope_tables(seq, HEAD_DIM)
    cos_t = jnp.concatenate([cos, cos], axis=1)
    sin_t = jnp.concatenate([-sin, sin], axis=1)
    tab_spec = ((tm, LANES), lambda i, j: (i % row_blocks, 0))
    qv, k_t = _proj(
        x2d, w_qkv.astype(BF16), tm=tm, tn=tn, mod=mod, seq=seq,
        extras=[(cos_t,) + tab_spec, (sin_t,) + tab_spec],
        outs=_qkv_out_specs(t, width, tm, tn),
        epilogue=functools.partial(_diff_qkv_epilogue, tn=tn, width=width, scale=scale),
        name="diff_qkv")
    blk = min(512, seq)
    dh = 2 * HEAD_DIM
    in_specs = [
        pl.BlockSpec((seq, dh), lambda b, h: (b, h)),
        pl.BlockSpec((dh, seq), lambda b, h: (h, b)),
        pl.BlockSpec((seq, dh), lambda b, h: (b, heads + h)),
        pl.BlockSpec((4, HEAD_DIM), lambda b, h: (0, 0)),
        pl.BlockSpec((1, dh), lambda b, h: (0, 0)),
    ]
    stat = lambda: pltpu.VMEM((blk, LANES), F32)
    accum = lambda: pltpu.VMEM((blk, dh), F32)
    return pl.pallas_call(
        functools.partial(_diff_attn_kernel, blk=blk, seq=seq, lam_init=lam_init),
        grid=(batch, heads),
        in_specs=in_specs,
        out_specs=pl.BlockSpec((seq, dh), lambda b, h: (b, h)),
        out_shape=jax.ShapeDtypeStruct((t, width), BF16),
        scratch_shapes=[stat(), stat(), accum(), stat(), stat(), accum(),
                        pltpu.VMEM((2, blk, blk), F32), pltpu.VMEM((2, blk, blk), F32)],
        compiler_params=_cparams(("parallel", "parallel"), 40),
        name="diff_attention",
    )(qv, k_t, qv, lam, subln_g.reshape(1, dh))


def _sb_qkv_epilogue(matmul, j, extra, outs, *, tn, width, scale):
    (wkt_ref,) = extra
    qv_ref, kt_ref = outs

    @pl.when(j * tn < width)
    def _():
        for c0, c1 in _column_parts(tn):
            qv_ref[:, c0:c1] = (matmul(c0, c1) * scale).astype(BF16)

    @pl.when(jnp.logical_and(j * tn >= width, j * tn < 2 * width))
    def _():
        kt_ref[...] = matmul.transposed(wkt_ref[...]).astype(BF16)

    @pl.when(j * tn >= 2 * width)
    def _():
        for c0, c1 in _column_parts(tn):
            qv_ref[:, c0:c1] = matmul(c0, c1).astype(BF16)


def _sb_mixer(x2d, mod, seq, batch, w_qkv):
    t, d = x2d.shape
    width = w_qkv.shape[1] // 3
    heads = width // HEAD_DIM
    scale = HEAD_DIM ** -0.5 * LOG2_E
    tm = min(512, seq)
    tn = min(1024, width)
    nq = width // tn
    w_bf16 = w_qkv.astype(BF16)
    w_kt = w_bf16[:, width:2 * width].T
    qv, k_t = _proj(
        x2d, w_bf16, tm=tm, tn=tn, mod=mod, seq=seq,
        extras=[(w_kt, (tn, d), lambda i, j: (jnp.clip(j - nq, 0, nq - 1), 0))],
        outs=_qkv_out_specs(t, width, tm, tn),
        epilogue=functools.partial(_sb_qkv_epilogue, tn=tn, width=width, scale=scale),
        name="sb_qkv")
    tq = min(512, seq)
    tk = min(256, seq)
    hg = ATTN_HEAD_GROUP
    groups = heads // hg
    return _attention(
        functools.partial(_sb_attn_kernel, tq=tq, tk=tk, seq=seq, heads=hg), qv, k_t, qv,
        batch=batch, seq=seq, groups=groups, k_off=0, v_off=groups,
        dq=hg * HEAD_DIM, dv=hg * HEAD_DIM, d_out=hg * HEAD_DIM,
        scratch=[pltpu.VMEM((hg, tq, LANES), F32), pltpu.VMEM((hg, tq, HEAD_DIM), F32)],
        vmem_mib=40, name="sb_attention")


def _split_bf16(v):
    hi = v.astype(BF16)
    return hi, (v - hi.astype(F32)).astype(BF16)


def _outproj_router_kernel(o_ref, w_ref, x_ref, mod_ref, g_ref, b_ref, rw_ref, rb_ref,
                           x1_ref, idx_ref, gate_ref, *, alpha, n_experts):
    y = jnp.dot(o_ref[...], w_ref[...], preferred_element_type=F32)
    mod = mod_ref[...]
    x1 = _layer_norm_rows(alpha * x_ref[...] + mod[2:3, :] * y, g_ref[...], b_ref[...])
    x1_ref[...] = x1
    h2 = x1 * (1.0 + mod[4:5, :]) + mod[3:4, :]
    hi, lo = _split_bf16(h2)
    logits = jnp.dot(jnp.concatenate([hi, lo, hi], axis=1), rw_ref[...],
                     preferred_element_type=F32) + rb_ref[...]
    lane = lax.broadcasted_iota(jnp.int32, logits.shape, 1).astype(F32)
    work = jnp.where(lane < n_experts, logits, -jnp.inf)
    idx_out = jnp.zeros(logits.shape, F32)
    val_out = jnp.zeros(logits.shape, F32)
    top = None
    denom = None
    for kk in range(TOP_K):
        m = jnp.max(work, axis=-1, keepdims=True)
        first = jnp.min(jnp.where(work == m, lane, float(LANES)), axis=-1, keepdims=True)
        if kk == 0:
            top = m
        e = jnp.exp(m - top)
        denom = e if kk == 0 else denom + e
        idx_out = jnp.where(lane == kk, first, idx_out)
        val_out = jnp.where(lane == kk, e, val_out)
        work = jnp.where(lane == first, -jnp.inf, work)
    idx_ref[...] = idx_out.astype(jnp.int32)
    gate_ref[...] = val_out / denom


def _outproj_router(o, w_o, x2d, mod, ln_g, ln_b, router_w, router_b, *, seq, alpha):
    t, d = x2d.shape
    n_experts = router_w.shape[1]
    tm = min(256, seq)
    blocks_per_seq = seq // tm
    once = pl.Buffered(1)
    rw = jnp.zeros((d, LANES), F32).at[:, :n_experts].set(router_w)
    rw_hi, rw_lo = _split_bf16(rw)
    rw3 = jnp.concatenate([rw_hi, rw_hi, rw_lo], axis=0)
    rb = jnp.zeros((1, LANES), F32).at[0, :n_experts].set(router_b)
    row = lambda i: (i, 0)
    fixed = lambda i: (0, 0)
    return pl.pallas_call(
        functools.partial(_outproj_router_kernel, alpha=alpha, n_experts=n_experts),
        grid=(t // tm,),
        in_specs=[
            pl.BlockSpec((tm, o.shape[1]), row),
            pl.BlockSpec(w_o.shape, fixed, pipeline_mode=once),
            pl.BlockSpec((tm, d), row),
            pl.BlockSpec((None, 6, d), lambda i: (i // blocks_per_seq, 0, 0)),
            pl.BlockSpec((1, d), fixed),
            pl.BlockSpec((1, d), fixed),
            pl.BlockSpec((3 * d, LANES), fixed, pipeline_mode=once),
            pl.BlockSpec((1, LANES), fixed),
        ],
        out_specs=[pl.BlockSpec((tm, d), row),
                   pl.BlockSpec((tm, LANES), row), pl.BlockSpec((tm, LANES), row)],
        out_shape=[jax.ShapeDtypeStruct((t, d), F32),
                   jax.ShapeDtypeStruct((t, LANES), jnp.int32), jax.ShapeDtypeStruct((t, LANES), F32)],
        compiler_params=_cparams(("parallel",), 48),
        name="outproj_ln_router",
    )(o, w_o.astype(BF16), x2d, mod, ln_g.reshape(1, d), ln_b.reshape(1, d), rw3, rb)


def _plan_kernel(idx_ref, rank_ref, cnt_ref, run_ref):
    @pl.when(pl.program_id(0) == 0)
    def _():
        run_ref[...] = jnp.zeros(run_ref.shape, F32)

    idx = idx_ref[...]
    tb = idx.shape[0]
    lane = lax.broadcasted_iota(jnp.int32, idx.shape, 1)
    onehots = [(lane == idx[:, kk:kk + 1]).astype(F32) for kk in range(TOP_K)]
    chosen = onehots[0]
    for kk in range(1, TOP_K):
        chosen = chosen + onehots[kk]
    earlier = (lax.broadcasted_iota(jnp.int32, (tb, tb), 1)
               < lax.broadcasted_iota(jnp.int32, (tb, tb), 0)).astype(BF16)
    prefix = jnp.dot(earlier, chosen.astype(BF16), preferred_element_type=F32) + run_ref[0:1, :]
    rank = jnp.zeros(idx.shape, F32)
    for kk in range(TOP_K):
        rank = jnp.where(lane == kk, jnp.sum(prefix * onehots[kk], axis=-1, keepdims=True), rank)
    rank_ref[...] = rank.astype(jnp.int32)
    run_ref[...] = run_ref[...] + jnp.sum(chosen, axis=0, keepdims=True)
    cnt_ref[...] = run_ref[...]


def _routing_plan(idx_pad, n_experts, tm):
    t = idx_pad.shape[0]
    tb = min(512, t)
    rank_pad, cnt = pl.pallas_call(
        _plan_kernel,
        grid=(t // tb,),
        in_specs=[pl.BlockSpec((tb, LANES), lambda i: (i, 0))],
        out_specs=[pl.BlockSpec((tb, LANES), lambda i: (i, 0)),
                   pl.BlockSpec((SUBLANES, LANES), lambda i: (0, 0))],
        out_shape=[jax.ShapeDtypeStruct((t, LANES), jnp.int32),
                   jax.ShapeDtypeStruct((SUBLANES, LANES), F32)],
        scratch_shapes=[pltpu.VMEM((SUBLANES, LANES), F32)],
        compiler_params=_cparams(("arbitrary",), 16),
        name="moe_plan",
    )(idx_pad)
    counts = cnt[0, :n_experts].astype(jnp.int32)
    padded = (counts + tm - 1) // tm * tm
    pad_end = jnp.cumsum(padded)
    pad_start = pad_end - padded
    chose = idx_pad[:, :TOP_K, None] == jnp.arange(n_experts, dtype=jnp.int32)
    dest = (jnp.sum(jnp.where(chose, pad_start, 0), axis=-1) + rank_pad[:, :TOP_K]).astype(jnp.int32)
    n_blocks = t * TOP_K // tm + n_experts
    block_start = jnp.arange(n_blocks, dtype=jnp.int32) * tm
    block_expert = jnp.minimum(jnp.sum(block_start[:, None] >= pad_end[None, :], axis=1),
                               n_experts - 1).astype(jnp.int32)
    n_used = (pad_end[-1] // tm).astype(jnp.int32).reshape(1)
    fill_start = jnp.concatenate([pad_start + counts, pad_end[-1:]]).astype(jnp.int32)
    fill_count = jnp.concatenate([pad_end - pad_start - counts,
                                  n_blocks * tm - pad_end[-1:]]).astype(jnp.int32)
    prev_expert = jnp.concatenate([jnp.full((1,), -1, jnp.int32), block_expert[:-1]])
    run_first = (block_expert != prev_expert).astype(jnp.int32)
    ids = jnp.arange(n_experts, dtype=jnp.int32)
    later_nonempty = jnp.logical_and(ids[None, :] > ids[:, None], counts[None, :] > 0)
    next_nonempty = jnp.min(jnp.where(later_nonempty, ids[None, :], n_experts), axis=1)
    run_next = jnp.where(next_nonempty < n_experts, next_nonempty, -1).astype(jnp.int32)[block_expert]
    return dict(dest=dest.reshape(-1), block_expert=block_expert, n_used=n_used, fill_start=fill_start,
                fill_count=fill_count, n_blocks=n_blocks, run_first=run_first, run_next=run_next)


def _row_pitch(chunks):
    return chunks + 1


def _pack_pairs(v):
    w = v.shape[1] // 2
    hi = lax.bitcast_convert_type(v[:, :w].astype(BF16).astype(F32), jnp.uint32)
    lo = lax.bitcast_convert_type(v[:, w:].astype(BF16).astype(F32), jnp.uint32)
    return hi | (lo >> 16)


def _unpack_pairs(u):
    hi = lax.bitcast_convert_type(u & jnp.uint32(0xFFFF0000), F32)
    lo = lax.bitcast_convert_type(u << 16, F32)
    return jnp.concatenate([hi, lo], axis=1)


def _load_rows(ref, lead, n_rows, chunks):
    pitch = _row_pitch(chunks)
    return jnp.concatenate(
        [ref[lead + (pl.ds(c, n_rows, stride=pitch), slice(None))] for c in range(chunks)], axis=1)


def _store_rows(ref, lead, val, chunks):
    pitch = _row_pitch(chunks)
    n_rows = val.shape[0]
    for c in range(chunks):
        ref[lead + (pl.ds(c, n_rows, stride=pitch), slice(None))] = val[:, c * LANES:(c + 1) * LANES]
    ref[lead + (pl.ds(chunks, n_rows, stride=pitch), slice(None))] = jnp.zeros((n_rows, LANES), val.dtype)


def _dispatch_kernel(dest_ref, fstart_ref, fcount_ref, x_ref, mod_ref, xs_hbm, hbuf, zrow, sem, fsem,
                     *, tb, n_steps, n_fills, chunks):
    i = pl.program_id(0)
    slot = i % 2
    pitch = _row_pitch(chunks)

    def drain(buf_slot):
        for _ in range(TOP_K):
            pltpu.make_async_copy(hbuf.at[buf_slot], xs_hbm.at[pl.ds(0, tb * pitch)],
                                  sem.at[buf_slot]).wait()

    def for_each_fill(fn):
        def per_range(f, c):
            start = fstart_ref[f]

            def per_row(j, c2):
                fn(pltpu.make_async_copy(zrow, xs_hbm.at[pl.ds((start + j) * pitch, pitch)], fsem))
                return c2

            lax.fori_loop(0, fcount_ref[f], per_row, 0)
            return c

        lax.fori_loop(0, n_fills, per_range, 0)

    @pl.when(i == 0)
    def _():
        zrow[...] = jnp.zeros(zrow.shape, zrow.dtype)
        for_each_fill(lambda copy: copy.start())

    @pl.when(i >= 2)
    def _():
        drain(slot)

    mod = mod_ref[...]
    h2 = x_ref[...] * (1.0 + mod[4:5, :]) + mod[3:4, :]
    _store_rows(hbuf, (slot,), _pack_pairs(h2), chunks)

    def per_token(r, c):
        base = (i * tb + r) * TOP_K
        src = hbuf.at[slot, pl.ds(r * pitch, pitch)]
        for kk in range(TOP_K):
            pltpu.make_async_copy(src, xs_hbm.at[pl.ds(dest_ref[base + kk] * pitch, pitch)],
                                  sem.at[slot]).start()
        return c

    lax.fori_loop(0, tb, per_token, 0, unroll=4)

    @pl.when(i == n_steps - 1)
    def _():
        drain(slot)
        if n_steps > 1:
            drain(1 - slot)
        for_each_fill(lambda copy: copy.wait())


def _dispatch(x1, mod, plan, *, seq, tm):
    dest, fill_start, fill_count = plan["dest"], plan["fill_start"], plan["fill_count"]
    n_blocks = plan["n_blocks"]
    t, d = x1.shape
    chunks = d // (2 * LANES)
    pitch = _row_pitch(chunks)
    tb = min(256, seq)
    n_steps = t // tb
    blocks_per_seq = seq // tb
    grid_spec = pltpu.PrefetchScalarGridSpec(
        num_scalar_prefetch=3,
        grid=(n_steps,),
        in_specs=[
            pl.BlockSpec((tb, d), lambda i, *_: (i, 0)),
            pl.BlockSpec((None, 6, d), lambda i, *_: (i // blocks_per_seq, 0, 0)),
        ],
        out_specs=pl.BlockSpec(memory_space=pl.ANY),
        scratch_shapes=[pltpu.VMEM((2, tb * pitch, LANES), jnp.uint32),
                        pltpu.VMEM((pitch, LANES), jnp.uint32),
                        pltpu.SemaphoreType.DMA((2,)), pltpu.SemaphoreType.DMA(())],
    )
    return pl.pallas_call(
        functools.partial(_dispatch_kernel, tb=tb, n_steps=n_steps, n_fills=fill_start.shape[0],
                          chunks=chunks),
        grid_spec=grid_spec,
        out_shape=jax.ShapeDtypeStruct((n_blocks * tm * pitch, LANES), jnp.uint32),
        compiler_params=_cparams(("arbitrary",), 32),
        name="moe_dispatch",
    )(dest, fill_start, fill_count, x1, mod)


def _expert_kernel(be_ref, used_ref, first_ref, next_ref, xs_ref, wgu_hbm, bgu_ref, wd_hbm, bd_ref,
                   y_ref, gu_stage, dn_stage, gu_bf16, dn_bf16, sem, *, layer, tm, d_expert, chunks):
    i = pl.program_id(0)

    def weight_copies(expert):
        return (pltpu.make_async_copy(wgu_hbm.at[layer, expert], gu_stage, sem.at[0]),
                pltpu.make_async_copy(wd_hbm.at[layer, expert], dn_stage, sem.at[1]))

    def cast(stage, out):
        n_rows = stage.shape[0]
        slab = math.gcd(n_rows, 256)

        def body(j, c):
            rows = pl.ds(pl.multiple_of(j * slab, slab), slab)
            out[rows, :] = stage[rows, :].astype(BF16)
            return c
        lax.fori_loop(0, n_rows // slab, body, 0)

    @pl.when(i == 0)
    def _():
        for copy in weight_copies(be_ref[0]):
            copy.start()

    @pl.when(jnp.logical_and(i < used_ref[0], first_ref[i] == 1))
    def _():
        for copy in weight_copies(be_ref[i]):
            copy.wait()
        cast(gu_stage, gu_bf16)
        cast(dn_stage, dn_bf16)

        @pl.when(next_ref[i] >= 0)
        def _():
            for copy in weight_copies(next_ref[i]):
                copy.start()

    @pl.when(i < used_ref[0])
    def _():
        x = _unpack_pairs(_load_rows(xs_ref, (), tm, chunks)).astype(BF16)
        gu = jnp.dot(x, gu_bf16[...], preferred_element_type=F32) + bgu_ref[...]
        g = jnp.minimum(gu[:, :d_expert], SWIGLU_LIMIT)
        lin = jnp.clip(gu[:, d_expert:], -SWIGLU_LIMIT, SWIGLU_LIMIT)
        act = g * jax.nn.sigmoid(SWIGLU_ALPHA * g) * (lin + 1.0)
        y = jnp.dot(act.astype(BF16), dn_bf16[...], preferred_element_type=F32) + bd_ref[...]
        _store_rows(y_ref, (), _pack_pairs(y), chunks)

    @pl.when(i >= used_ref[0])
    def _():
        y_ref[...] = jnp.zeros(y_ref.shape, y_ref.dtype)


def _experts(xs, plan, layer, w_gate_up, b_gate_up, w_down, b_down, tm):
    n_blocks = plan["n_blocks"]
    depth, n_experts, d_expert, d = w_down.shape
    chunks = d // (2 * LANES)
    pitch = _row_pitch(chunks)
    per_expert = lambda i, be, used, first, nxt: (layer, be[i], 0, 0)
    rows = lambda i, be, used, first, nxt: (i, 0)
    used_rows = lambda i, be, used, first, nxt: (jnp.minimum(i, used[0] - 1), 0)
    grid_spec = pltpu.PrefetchScalarGridSpec(
        num_scalar_prefetch=4,
        grid=(n_blocks,),
        in_specs=[
            pl.BlockSpec((tm * pitch, LANES), used_rows),
            pl.BlockSpec(memory_space=pl.ANY),
            pl.BlockSpec((None, None, 1, 2 * d_expert), per_expert),
            pl.BlockSpec(memory_space=pl.ANY),
            pl.BlockSpec((None, None, 1, d), per_expert),
        ],
        out_specs=pl.BlockSpec((tm * pitch, LANES), rows),
        scratch_shapes=[pltpu.VMEM((d, 2 * d_expert), F32), pltpu.VMEM((d_expert, d), F32),
                        pltpu.VMEM((d, 2 * d_expert), BF16), pltpu.VMEM((d_expert, d), BF16),
                        pltpu.SemaphoreType.DMA((2,))],
    )
    return pl.pallas_call(
        functools.partial(_expert_kernel, layer=layer, tm=tm, d_expert=d_expert, chunks=chunks),
        grid_spec=grid_spec,
        out_shape=jax.ShapeDtypeStruct((n_blocks * tm * pitch, LANES), jnp.uint32),
        compiler_params=_cparams(("arbitrary",), 56),
        name="moe_experts",
    )(plan["block_expert"], plan["n_used"], plan["run_first"], plan["run_next"], xs, w_gate_up,
      b_gate_up.reshape(depth, n_experts, 1, 2 * d_expert), w_down,
      b_down.reshape(depth, n_experts, 1, d))


def _combine_kernel(dest_ref, ys_hbm, x_ref, gate_ref, mod_ref, g_ref, b_ref, o_ref, ybuf, sem,
                    *, tn, n_blocks, alpha, chunks):
    i = pl.program_id(0)
    slot = i % 2
    pitch = _row_pitch(chunks)

    def issue(block, buf_slot):
        def body(r, c):
            base = (block * tn + r) * TOP_K
            for kk in range(TOP_K):
                pltpu.make_async_copy(ys_hbm.at[pl.ds(dest_ref[base + kk] * pitch, pitch)],
                                      ybuf.at[buf_slot, kk, pl.ds(r * pitch, pitch)],
                                      sem.at[buf_slot]).start()
            return c
        lax.fori_loop(0, tn, body, 0, unroll=4)

    @pl.when(i == 0)
    def _():
        issue(0, 0)

    @pl.when(i + 1 < n_blocks)
    def _():
        issue(i + 1, 1 - slot)

    for kk in range(TOP_K):
        pltpu.make_async_copy(ys_hbm.at[pl.ds(0, tn * pitch)], ybuf.at[slot, kk], sem.at[slot]).wait()

    gate = gate_ref[...]
    y = None
    for kk in range(TOP_K):
        term = gate[:, kk:kk + 1] * _unpack_pairs(_load_rows(ybuf, (slot, kk), tn, chunks))
        y = term if y is None else y + term
    mod = mod_ref[...]
    o_ref[...] = _layer_norm_rows(alpha * x_ref[...] + mod[5:6, :] * y, g_ref[...], b_ref[...])


def _combine(ys, dest, x1, gate_pad, mod, ln_g, ln_b, *, seq, alpha):
    t, d = x1.shape
    chunks = d // (2 * LANES)
    pitch = _row_pitch(chunks)
    tn = min(256, seq)
    n_blocks = t // tn
    blocks_per_seq = seq // tn
    grid_spec = pltpu.PrefetchScalarGridSpec(
        num_scalar_prefetch=1,
        grid=(n_blocks,),
        in_specs=[
            pl.BlockSpec(memory_space=pl.ANY),
            pl.BlockSpec((tn, d), lambda i, dst: (i, 0)),
            pl.BlockSpec((tn, LANES), lambda i, dst: (i, 0)),
            pl.BlockSpec((None, 6, d), lambda i, dst: (i // blocks_per_seq, 0, 0)),
            pl.BlockSpec((1, d), lambda i, dst: (0, 0)),
            pl.BlockSpec((1, d), lambda i, dst: (0, 0)),
        ],
        out_specs=pl.BlockSpec((tn, d), lambda i, dst: (i, 0)),
        scratch_shapes=[pltpu.VMEM((2, TOP_K, tn * pitch, LANES), jnp.uint32), pltpu.SemaphoreType.DMA((2,))],
    )
    return pl.pallas_call(
        functools.partial(_combine_kernel, tn=tn, n_blocks=n_blocks, alpha=alpha, chunks=chunks),
        grid_spec=grid_spec,
        out_shape=jax.ShapeDtypeStruct((t, d), F32),
        compiler_params=_cparams(("arbitrary",), 40),
        name="moe_combine_ln",
    )(dest, ys, x1, gate_pad, mod, ln_g.reshape(1, d), ln_b.reshape(1, d))


def kernel(x, c, ada_w, ada_b, ln_g, ln_b, mla_w_in, mla_q_norm_g, mla_kv_norm_g, mla_w_uq, mla_w_ukv,
           mla_w_o, diff_w_qkv, diff_lambda, diff_subln_g, diff_w_o, sb_w_qkv, sb_w_o, moe_router_w,
           moe_router_b, moe_w_gate_up, moe_b_gate_up, moe_w_down, moe_b_down):
    batch, seq, d = x.shape
    depth = ada_w.shape[0]
    alpha = (2 * depth) ** 0.25
    n_experts = moe_router_w.shape[-1]
    moe_tm = 256
    mods = _adaln(c, ada_w, ada_b)
    x2d = x.reshape(batch * seq, d)
    for i in range(depth):
        mod = mods[i]
        kind, j = i % N_MIXERS, i // N_MIXERS
        if kind == 0:
            o = _mla_mixer(x2d, mod, seq, batch, mla_w_in[j], mla_q_norm_g[j], mla_kv_norm_g[j],
                           mla_w_uq[j], mla_w_ukv[j])
            w_o = mla_w_o[j]
        elif kind == 1:
            o = _diff_mixer(x2d, mod, seq, batch, i, diff_w_qkv[j], diff_lambda[j], diff_subln_g[j])
            w_o = diff_w_o[j]
        else:
            o = _sb_mixer(x2d, mod, seq, batch, sb_w_qkv[j])
            w_o = sb_w_o[j]
        x1, idx_pad, gate_pad = _outproj_router(
            o, w_o, x2d, mod, ln_g[i, 0], ln_b[i, 0], moe_router_w[i], moe_router_b[i],
            seq=seq, alpha=alpha)
        plan = _routing_plan(idx_pad, n_experts, moe_tm)
        xs = _dispatch(x1, mod, plan, seq=seq, tm=moe_tm)
        ys = _experts(xs, plan, i, moe_w_gate_up, moe_b_gate_up, moe_w_down, moe_b_down, moe_tm)
        x2d = _combine(ys, plan["dest"], x1, gate_pad, mod, ln_g[i, 1], ln_b[i, 1], seq=seq,
                       alpha=alpha)
    return x2d.reshape(batch, seq, d)
```

```python
import functools
import math

import jax
import jax.numpy as jnp
from jax import lax
from jax.experimental import pallas as pl
from jax.experimental.pallas import tpu as pltpu

F32 = jnp.float32
BF16 = jnp.bfloat16

N_MIXERS = 3
NEG_INF = -1e30
ROPE_THETA = 10000.0
LN_EPS = 1e-5
RMS_EPS = 1e-6
HEAD_DIM = 128
MLA_ROPE_DIM = 64
TOP_K = 4
SWIGLU_LIMIT = 7.0
SWIGLU_ALPHA = 1.702
LANES = 128
SUBLANES = 8
LOG2_E = 1.4426950408889634
ATTN_HEAD_GROUP = 2
PROJ_PART = 512
SB_DEAD_LOG2 = -160.0
MIB = 1024 * 1024


def _cparams(sems, vmem_mib):
    return pltpu.CompilerParams(dimension_semantics=sems, vmem_limit_bytes=vmem_mib * MIB)


def _layer_norm_rows(r, g, b):
    mu = jnp.mean(r, axis=-1, keepdims=True)
    d = r - mu
    var = jnp.mean(d * d, axis=-1, keepdims=True)
    return d * lax.rsqrt(var + LN_EPS) * g + b


def _adaln_kernel(c_ref, w_ref, b_ref, o_ref):
    c = c_ref[...]
    cond = c * jax.nn.sigmoid(c)
    o_ref[...] = jnp.dot(cond.astype(BF16), w_ref[...].astype(BF16),
                         preferred_element_type=F32) + b_ref[...]


def _adaln(c, ada_w, ada_b):
    depth, d, n = ada_w.shape
    b = c.shape[0]
    rows = 16
    tn = next(cand for cand in (1024, 512, 256, 128) if n % cand == 0)
    c_pad = jnp.zeros((rows, d), F32).at[:b].set(c)
    out = pl.pallas_call(
        _adaln_kernel,
        grid=(depth, n // tn),
        in_specs=[
            pl.BlockSpec((rows, d), lambda l, j: (0, 0)),
            pl.BlockSpec((None, d, tn), lambda l, j: (l, 0, j)),
            pl.BlockSpec((None, 1, tn), lambda l, j: (l, 0, j)),
        ],
        out_specs=pl.BlockSpec((None, rows, tn), lambda l, j: (l, 0, j)),
        out_shape=jax.ShapeDtypeStruct((depth, rows, n), F32),
        compiler_params=_cparams(("parallel", "parallel"), 40),
        name="adaln",
    )(c_pad, ada_w, ada_b.reshape(depth, 1, n))
    return out[:, :b].reshape(depth, b, 6, d)


def _proj_kernel(*refs, has_mod, n_extra, n_out, epilogue):
    x_ref = refs[0]
    pos = 1
    mod_ref = None
    if has_mod:
        mod_ref = refs[pos]
        pos += 1
    w_ref = refs[pos]
    pos += 1
    extra = refs[pos:pos + n_extra]
    pos += n_extra
    outs = refs[pos:pos + n_out]
    pos += n_out
    if has_mod:
        h_ref = refs[pos]

        @pl.when(pl.program_id(1) == 0)
        def _():
            mod = mod_ref[...]
            h_ref[...] = (x_ref[...] * (1.0 + mod[1:2, :]) + mod[0:1, :]).astype(BF16)

        h = h_ref[...]
    else:
        h = x_ref[...]
    def matmul(c0, c1):
        return jnp.dot(h, w_ref[:, c0:c1], preferred_element_type=F32)

    matmul.transposed = lambda w_t: _dot_nt(w_t, h)
    epilogue(matmul, pl.program_id(1), extra, outs)


def _proj(x, w, *, tm, tn, mod, seq, extras, outs, epilogue, vmem_mib=48, name):
    t, k = x.shape
    n = w.shape[1]
    in_specs = [pl.BlockSpec((tm, k), lambda i, j: (i, 0))]
    args = [x]
    if mod is not None:
        blocks_per_seq = seq // tm
        in_specs.append(pl.BlockSpec((None, 6, k), lambda i, j: (i // blocks_per_seq, 0, 0)))
        args.append(mod)
    in_specs.append(pl.BlockSpec((k, tn), lambda i, j: (0, j)))
    args.append(w)
    for arr, blk, imap in extras:
        in_specs.append(pl.BlockSpec(blk, imap))
        args.append(arr)
    out_specs = [pl.BlockSpec(blk, imap) for _, _, blk, imap in outs]
    out_shape = [jax.ShapeDtypeStruct(shp, dt) for shp, dt, _, _ in outs]
    scratch = [pltpu.VMEM((tm, k), BF16)] if mod is not None else []
    kern = functools.partial(_proj_kernel, has_mod=mod is not None, n_extra=len(extras),
                             n_out=len(outs), epilogue=epilogue)
    return pl.pallas_call(
        kern,
        grid=(t // tm, n // tn),
        in_specs=in_specs,
        out_specs=out_specs,
        out_shape=out_shape,
        scratch_shapes=scratch,
        compiler_params=_cparams(("parallel", "arbitrary"), vmem_mib),
        name=name,
    )(*args)


def _rope_tables(seq, dim):
    half = dim // 2
    inv_freq = ROPE_THETA ** (-jnp.arange(half, dtype=F32) * (2.0 / dim))
    ang = jnp.arange(seq, dtype=F32)[:, None] * inv_freq[None, :]
    return jnp.cos(ang), jnp.sin(ang)


def _mla_rope_tables(seq, scale):
    cos, sin = _rope_tables(seq, MLA_ROPE_DIM)
    z32 = jnp.zeros_like(cos)
    z64 = jnp.zeros((seq, LANES - MLA_ROPE_DIM), F32)
    c = jnp.concatenate([cos, cos, z64], axis=1) * scale
    s_left = jnp.concatenate([-sin, z32, z64], axis=1) * scale
    s_right = jnp.concatenate([z32, sin, z64], axis=1) * scale
    return c, s_left, s_right


def _rope64_in_chunk(r, c, s_left, s_right):
    half = MLA_ROPE_DIM // 2
    return (r * c + pltpu.roll(r, LANES - half, 1) * s_left + pltpu.roll(r, half, 1) * s_right)


def _mla_lat_epilogue(matmul, j, extra, outs, *, q_rank, kv_rank):
    gq_ref, gkv_ref, c_ref, sl_ref, sr_ref = extra
    cq_ref, ckv_ref, kr_ref = outs
    acc = matmul(0, q_rank + kv_rank + LANES)

    def rms(v, g):
        return v * lax.rsqrt(jnp.mean(v * v, axis=-1, keepdims=True) + RMS_EPS) * g

    cq_ref[...] = rms(acc[:, :q_rank], gq_ref[...]).astype(BF16)
    ckv_ref[...] = rms(acc[:, q_rank:q_rank + kv_rank], gkv_ref[...]).astype(BF16)
    kr = acc[:, q_rank + kv_rank:]
    kr_ref[...] = _rope64_in_chunk(kr, c_ref[...], sl_ref[...], sr_ref[...]).astype(BF16)


def _mla_q_epilogue(matmul, j, extra, outs, *, heads_per_tile, scale):
    c_ref, sl_ref, sr_ref = extra
    (q_ref,) = outs
    c, sl, sr = c_ref[...], sl_ref[...], sr_ref[...]
    for h in range(heads_per_tile):
        lo = h * 2 * LANES
        acc = matmul(lo, lo + 2 * LANES)
        q_ref[:, lo:lo + LANES] = (acc[:, :LANES] * scale).astype(BF16)
        q_ref[:, lo + LANES:lo + 2 * LANES] = _rope64_in_chunk(acc[:, LANES:], c, sl, sr).astype(BF16)


def _mla_kv_kernel(ckv_ref, wkt_ref, wv_ref, kr_ref, kt_ref, v_ref, *, heads_per_tile):
    x = ckv_ref[...]
    kn_t = _dot_nt(wkt_ref[...], x)
    vv = jnp.dot(x, wv_ref[...], preferred_element_type=F32)
    kr_t = kr_ref[...].astype(F32).T.astype(BF16)
    ones = jnp.ones((x.shape[0], LANES), BF16)
    for h in range(heads_per_tile):
        lo = h * 2 * LANES
        kt_ref[lo:lo + LANES, :] = kn_t[h * LANES:(h + 1) * LANES, :].astype(BF16)
        kt_ref[lo + LANES:lo + 2 * LANES, :] = kr_t
        v_ref[:, lo:lo + LANES] = vv[:, h * LANES:(h + 1) * LANES].astype(BF16)
        v_ref[:, lo + LANES:lo + 2 * LANES] = ones


def _causal_mask(tq, tk):
    row = lax.broadcasted_iota(jnp.int32, (tq, tk), 0)
    col = lax.broadcasted_iota(jnp.int32, (tq, tk), 1)
    return col <= row


def _dot_nt(a, b):
    return lax.dot_general(a, b, (((1,), (1,)), ((), ())), preferred_element_type=F32)


def _qk(q, k_t):
    return jnp.dot(q, k_t, preferred_element_type=F32)


def _softmax_step(s, v, m_ref, l_ref, acc_ref):
    m_prev = m_ref[...]
    m_new = jnp.maximum(m_prev, jnp.max(s, axis=-1, keepdims=True))
    p = jnp.exp2(s - jnp.tile(m_new, (1, s.shape[1] // LANES)))
    alpha = jnp.exp2(m_prev - m_new)
    if l_ref is not None:
        l_ref[...] = alpha * l_ref[...] + jnp.sum(p, axis=-1, keepdims=True)
    acc_ref[...] = (acc_ref[...] * jnp.tile(alpha, (1, acc_ref.shape[1] // LANES))
                    + jnp.dot(p.astype(BF16), v, preferred_element_type=F32))
    m_ref[...] = m_new


def _mla_attn_kernel(q_ref, k_ref, v_ref, o_ref, m_ref, acc_ref, sa_ref, sb_ref, *, blk, seq, heads):
    w = 2 * LANES

    def q_body(qi, carry):
        q0 = pl.multiple_of(qi * blk, blk)
        qs = [q_ref[pl.ds(q0, blk), h * w:(h + 1) * w] for h in range(heads)]
        m_ref[...] = jnp.full(m_ref.shape, NEG_INF, F32)
        acc_ref[...] = jnp.zeros(acc_ref.shape, F32)

        def score_tile(s_ref, tile):
            k0 = pl.multiple_of(tile * blk, blk)
            keep = jnp.logical_or(_causal_mask(blk, blk), tile < qi)
            for h in range(heads):
                s = _qk(qs[h], k_ref[h * w:(h + 1) * w, pl.ds(k0, blk)])
                s_ref[h] = jnp.where(keep, s, NEG_INF)

        def consume(s_ref, tile):
            k0 = pl.multiple_of(tile * blk, blk)
            for h in range(heads):
                _softmax_step(s_ref[h], v_ref[pl.ds(k0, blk), h * w:(h + 1) * w], m_ref.at[h], None,
                              acc_ref.at[h])

        score_tile(sa_ref, 0)

        def pair_body(jj, c):
            score_tile(sb_ref, 2 * jj + 1)
            consume(sa_ref, 2 * jj)
            score_tile(sa_ref, 2 * jj + 2)
            consume(sb_ref, 2 * jj + 1)
            return c

        lax.fori_loop(0, qi // 2, pair_body, 0)

        @pl.when(qi % 2 == 1)
        def _():
            score_tile(sb_ref, qi)
            consume(sa_ref, qi - 1)
            consume(sb_ref, qi)

        @pl.when(qi % 2 == 0)
        def _():
            consume(sa_ref, qi)

        for h in range(heads):
            acc = acc_ref[h]
            o_ref[pl.ds(q0, blk), h * HEAD_DIM:(h + 1) * HEAD_DIM] = (
                acc[:, :HEAD_DIM] / acc[:, HEAD_DIM:]).astype(o_ref.dtype)
        return carry

    lax.fori_loop(0, seq // blk, q_body, 0)


def _diff_attn_kernel(q_ref, k_ref, v_ref, lam_ref, g_ref, o_ref,
                      m1_ref, l1_ref, a1_ref, m2_ref, l2_ref, a2_ref, sa_ref, sb_ref,
                      *, blk, seq, lam_init):
    hd = HEAD_DIM
    lf = lam_ref[...]
    lam_full = (jnp.exp(jnp.sum(lf[0:1, :] * lf[1:2, :], axis=-1, keepdims=True))
                - jnp.exp(jnp.sum(lf[2:3, :] * lf[3:4, :], axis=-1, keepdims=True)) + lam_init)

    def q_body(qi, carry):
        q0 = pl.multiple_of(qi * blk, blk)
        q = q_ref[pl.ds(q0, blk), :]
        q1, q2 = q[:, :hd], q[:, hd:]
        for m_ref, l_ref, a_ref in ((m1_ref, l1_ref, a1_ref), (m2_ref, l2_ref, a2_ref)):
            m_ref[...] = jnp.full(m_ref.shape, NEG_INF, F32)
            l_ref[...] = jnp.zeros(l_ref.shape, F32)
            a_ref[...] = jnp.zeros(a_ref.shape, F32)

        def score_tile(s_ref, tile):
            k0 = pl.multiple_of(tile * blk, blk)
            keep = jnp.logical_or(_causal_mask(blk, blk), tile < qi)
            s_ref[0] = jnp.where(keep, _qk(q1, k_ref[:hd, pl.ds(k0, blk)]), NEG_INF)
            s_ref[1] = jnp.where(keep, _qk(q2, k_ref[hd:, pl.ds(k0, blk)]), NEG_INF)

        def consume(s_ref, tile):
            v = v_ref[pl.ds(pl.multiple_of(tile * blk, blk), blk), :]
            _softmax_step(s_ref[0], v, m1_ref, l1_ref, a1_ref)
            _softmax_step(s_ref[1], v, m2_ref, l2_ref, a2_ref)

        score_tile(sa_ref, 0)

        def pair_body(jj, c):
            score_tile(sb_ref, 2 * jj + 1)
            consume(sa_ref, 2 * jj)
            score_tile(sa_ref, 2 * jj + 2)
            consume(sb_ref, 2 * jj + 1)
            return c

        lax.fori_loop(0, qi // 2, pair_body, 0)

        @pl.when(qi % 2 == 1)
        def _():
            score_tile(sb_ref, qi)
            consume(sa_ref, qi - 1)
            consume(sb_ref, qi)

        @pl.when(qi % 2 == 0)
        def _():
            consume(sa_ref, qi)

        reps = (1, 2 * hd // LANES)
        o = (a1_ref[...] / jnp.tile(l1_ref[...], reps)
             - lam_full * (a2_ref[...] / jnp.tile(l2_ref[...], reps)))
        o = o * lax.rsqrt(jnp.mean(o * o, axis=-1, keepdims=True) + LN_EPS) * g_ref[...]
        o_ref[pl.ds(q0, blk), :] = (o * (1.0 - lam_init)).astype(o_ref.dtype)
        return carry

    lax.fori_loop(0, seq // blk, q_body, 0)


def _sb_attn_kernel(q_ref, k_ref, v_ref, o_ref, run_ref, acc_ref, *, tq, tk, seq, heads):
    chunks_per_q = tq // tk
    hd = HEAD_DIM
    jr = lax.broadcasted_iota(jnp.int32, (2 * tk, tk), 0)
    jc = lax.broadcasted_iota(jnp.int32, (2 * tk, tk), 1)
    later_mat = (jnp.where(jr >= tk, jr - tk, jr) > jc).astype(BF16)

    def q_body(qi, carry):
        q0 = pl.multiple_of(qi * tq, tq)
        qs = [q_ref[pl.ds(q0, tq), h * hd:(h + 1) * hd] for h in range(heads)]
        run_ref[...] = jnp.zeros(run_ref.shape, F32)
        acc_ref[...] = jnp.zeros(acc_ref.shape, F32)

        def step(k0, diag_offset):
            logits = [_qk(qs[h], k_ref[h * hd:(h + 1) * hd, pl.ds(k0, tk)]) for h in range(heads)]
            for h in range(heads):
                z = logits[h]
                log_beta = jnp.minimum(z, 0.0) - jnp.log2(1.0 + jnp.exp2(-jnp.abs(z)))
                log_keep = log_beta - z
                if diag_offset is not None:
                    row = lax.broadcasted_iota(jnp.int32, (tq, tk), 0)
                    col = lax.broadcasted_iota(jnp.int32, (tq, tk), 1) + diag_offset
                    strict = col < row
                    log_keep = jnp.where(strict, log_keep, 0.0)
                hi = log_keep.astype(BF16)
                lo = (log_keep - hi.astype(F32)).astype(BF16)
                run = run_ref[h]
                later = (jnp.dot(jnp.concatenate([hi, lo], axis=1), later_mat,
                                 preferred_element_type=F32) + jnp.tile(run, (1, tk // LANES)))
                a = jnp.exp2(log_beta + later)
                if diag_offset is not None:
                    a = jnp.where(strict, a, 0.0)
                acc_ref[h] += jnp.dot(a.astype(BF16), v_ref[pl.ds(k0, tk), h * hd:(h + 1) * hd],
                                      preferred_element_type=F32)
                run_ref[h] = run + jnp.sum(log_keep, axis=-1, keepdims=True)

        for c in reversed(range(chunks_per_q)):
            step(q0 + c * tk, c * tk)

        def any_live():
            run_max = run_ref[0]
            for h in range(1, heads):
                run_max = jnp.maximum(run_max, run_ref[h])
            return jnp.max(run_max) > SB_DEAD_LOG2

        def kv_cond(carry):
            n, live = carry
            return jnp.logical_and(n < qi * chunks_per_q, live)

        def kv_body(carry):
            n, _ = carry
            kj = qi * chunks_per_q - 1 - n
            step(pl.multiple_of(kj * tk, tk), None)
            return n + 1, any_live()

        lax.while_loop(kv_cond, kv_body, (jnp.int32(0), any_live()))
        for h in range(heads):
            o_ref[pl.ds(q0, tq), h * hd:(h + 1) * hd] = acc_ref[h].astype(o_ref.dtype)
        return carry

    lax.fori_loop(0, seq // tq, q_body, 0)


def _attention(kernel, q, k, v, *, batch, seq, groups, k_off, v_off, dq, dv, d_out, scratch,
               vmem_mib, name):
    in_specs = [
        pl.BlockSpec((seq, dq), lambda b, g: (b, g)),
        pl.BlockSpec((dq, seq), lambda b, g: (k_off + g, b)),
        pl.BlockSpec((seq, dv), lambda b, g: (b, v_off + g)),
    ]
    return pl.pallas_call(
        kernel,
        grid=(batch, groups),
        in_specs=in_specs,
        out_specs=pl.BlockSpec((seq, d_out), lambda b, g: (b, g)),
        out_shape=jax.ShapeDtypeStruct((batch * seq, groups * d_out), BF16),
        scratch_shapes=scratch,
        compiler_params=_cparams(("parallel", "parallel"), vmem_mib),
        name=name,
    )(q, k, v)


def _mla_mixer(x2d, mod, seq, batch, w_in, q_norm_g, kv_norm_g, w_uq, w_ukv):
    t, d = x2d.shape
    q_rank, kv_rank = q_norm_g.shape[0], kv_norm_g.shape[0]
    heads = w_uq.shape[1] // (HEAD_DIM + MLA_ROPE_DIM)
    scale = (HEAD_DIM + MLA_ROPE_DIM) ** -0.5 * LOG2_E
    tm = min(512, seq)
    row_blocks = seq // tm

    lat_n = q_rank + kv_rank + LANES
    w_in_p = jnp.zeros((d, lat_n), BF16).at[:, :w_in.shape[1]].set(w_in.astype(BF16))
    c_k, sl_k, sr_k = _mla_rope_tables(seq, 1.0)
    c_q, sl_q, sr_q = _mla_rope_tables(seq, scale)
    tab_spec = ((tm, LANES), lambda i, j: (i % row_blocks, 0))

    cq, ckv, kr = _proj(
        x2d, w_in_p, tm=tm, tn=lat_n, mod=mod, seq=seq,
        extras=[(q_norm_g.reshape(1, q_rank), (1, q_rank), lambda i, j: (0, 0)),
                (kv_norm_g.reshape(1, kv_rank), (1, kv_rank), lambda i, j: (0, 0)),
                (c_k,) + tab_spec, (sl_k,) + tab_spec, (sr_k,) + tab_spec],
        outs=[((t, q_rank), BF16, (tm, q_rank), lambda i, j: (i, 0)),
              ((t, kv_rank), BF16, (tm, kv_rank), lambda i, j: (i, 0)),
              ((t, LANES), BF16, (tm, LANES), lambda i, j: (i, 0))],
        epilogue=functools.partial(_mla_lat_epilogue, q_rank=q_rank, kv_rank=kv_rank),
        name="mla_latent")

    w_q = w_uq.reshape(q_rank, heads, HEAD_DIM + MLA_ROPE_DIM).astype(BF16)
    w_q = jnp.pad(w_q, ((0, 0), (0, 0), (0, 2 * LANES - HEAD_DIM - MLA_ROPE_DIM)))
    w_q = w_q.reshape(q_rank, heads * 2 * LANES)
    hpt = min(heads, 8)
    (q_cat,) = _proj(
        cq, w_q, tm=tm, tn=hpt * 2 * LANES, mod=None, seq=seq,
        extras=[(c_q,) + tab_spec, (sl_q,) + tab_spec, (sr_q,) + tab_spec],
        outs=[((t, heads * 2 * LANES), BF16, (tm, hpt * 2 * LANES), lambda i, j: (i, j))],
        epilogue=functools.partial(_mla_q_epilogue, heads_per_tile=hpt, scale=scale),
        name="mla_q_up")

    w_kv = w_ukv.reshape(kv_rank, heads, 2 * HEAD_DIM).astype(BF16)
    w_kt = w_kv[:, :, :HEAD_DIM].reshape(kv_rank, heads * HEAD_DIM).T
    w_v = w_kv[:, :, HEAD_DIM:].reshape(kv_rank, heads * HEAD_DIM)
    k_cat, v = pl.pallas_call(
        functools.partial(_mla_kv_kernel, heads_per_tile=hpt),
        grid=(t // tm, heads // hpt),
        in_specs=[
            pl.BlockSpec((tm, kv_rank), lambda i, j: (i, 0)),
            pl.BlockSpec((hpt * HEAD_DIM, kv_rank), lambda i, j: (j, 0)),
            pl.BlockSpec((kv_rank, hpt * HEAD_DIM), lambda i, j: (0, j)),
            pl.BlockSpec((tm, LANES), lambda i, j: (i, 0)),
        ],
        out_specs=[
            pl.BlockSpec((hpt * 2 * LANES, tm), lambda i, j: (j, i)),
            pl.BlockSpec((tm, hpt * 2 * LANES), lambda i, j: (i, j)),
        ],
        out_shape=[jax.ShapeDtypeStruct((heads * 2 * LANES, t), BF16),
                   jax.ShapeDtypeStruct((t, heads * 2 * LANES), BF16)],
        compiler_params=_cparams(("parallel", "parallel"), 32),
        name="mla_kv_up",
    )(ckv, w_kt, w_v, kr)

    blk = min(512, seq)
    hg = ATTN_HEAD_GROUP
    return _attention(
        functools.partial(_mla_attn_kernel, blk=blk, seq=seq, heads=hg), q_cat, k_cat, v,
        batch=batch, seq=seq, groups=heads // hg, k_off=0, v_off=0,
        dq=hg * 2 * LANES, dv=hg * 2 * LANES, d_out=hg * HEAD_DIM,
        scratch=[pltpu.VMEM((hg, blk, LANES), F32), pltpu.VMEM((hg, blk, 2 * LANES), F32),
                 pltpu.VMEM((hg, blk, blk), F32), pltpu.VMEM((hg, blk, blk), F32)],
        vmem_mib=48, name="mla_attention")


def _qkv_out_specs(t, width, tm, tn):
    nq = width // tn

    def qv_map(i, j):
        return i, jnp.where(j < nq, j, jnp.where(j < 2 * nq, nq - 1, j - nq))

    def kt_map(i, j):
        return jnp.clip(j - nq, 0, nq - 1), i

    return [((t, 2 * width), BF16, (tm, tn), qv_map), ((width, t), BF16, (tn, tm), kt_map)]


def _column_parts(tn):
    part = min(tn, PROJ_PART)
    return [(c0, c0 + part) for c0 in range(0, tn, part)]


def _diff_qkv_epilogue(matmul, j, extra, outs, *, tn, width, scale):
    cos_ref, sin_ref = extra
    qv_ref, kt_ref = outs

    def rope(acc, factor):
        cos = cos_ref[...] * factor
        sin = sin_ref[...] * factor
        chunks = []
        for c in range(acc.shape[1] // LANES):
            a = acc[:, c * LANES:(c + 1) * LANES]
            chunks.append(a * cos + pltpu.roll(a, HEAD_DIM // 2, 1) * sin)
        return jnp.concatenate(chunks, axis=1)

    @pl.when(j * tn < width)
    def _():
        for c0, c1 in _column_parts(tn):
            qv_ref[:, c0:c1] = rope(matmul(c0, c1), scale).astype(BF16)

    @pl.when(jnp.logical_and(j * tn >= width, j * tn < 2 * width))
    def _():
        for c0, c1 in _column_parts(tn):
            kt_ref[c0:c1, :] = rope(matmul(c0, c1), 1.0).T.astype(BF16)

    @pl.when(j * tn >= 2 * width)
    def _():
        for c0, c1 in _column_parts(tn):
            qv_ref[:, c0:c1] = matmul(c0, c1).astype(BF16)


def _diff_mixer(x2d, mod, seq, batch, layer_idx, w_qkv, lam, subln_g):
    t, d = x2d.shape
    width = w_qkv.shape[1] // 3
    heads = width // (2 * HEAD_DIM)
    scale = HEAD_DIM ** -0.5 * LOG2_E
    lam_init = 0.8 - 0.6 * math.exp(-0.3 * layer_idx)
    tm = min(512, seq)
    tn = min(1024, width)
    row_blocks = seq // tm
    cos, sin = _rope_tables(seq, HEAD_DIM)
    cos_t = jnp.concatenate([cos, cos], axis=1)
    sin_t = jnp.concatenate([-sin, sin], axis=1)
    tab_spec = ((tm, LANES), lambda i, j: (i % row_blocks, 0))
    qv, k_t = _proj(
        x2d, w_qkv.astype(BF16), tm=tm, tn=tn, mod=mod, seq=seq,
        extras=[(cos_t,) + tab_spec, (sin_t,) + tab_spec],
        outs=_qkv_out_specs(t, width, tm, tn),
        epilogue=functools.partial(_diff_qkv_epilogue, tn=tn, width=width, scale=scale),
        name="diff_qkv")
    blk = min(512, seq)
    dh = 2 * HEAD_DIM
    in_specs = [
        pl.BlockSpec((seq, dh), lambda b, h: (b, h)),
        pl.BlockSpec((dh, seq), lambda b, h: (h, b)),
        pl.BlockSpec((seq, dh), lambda b, h: (b, heads + h)),
        pl.BlockSpec((4, HEAD_DIM), lambda b, h: (0, 0)),
        pl.BlockSpec((1, dh), lambda b, h: (0, 0)),
    ]
    stat = lambda: pltpu.VMEM((blk, LANES), F32)
    accum = lambda: pltpu.VMEM((blk, dh), F32)
    return pl.pallas_call(
        functools.partial(_diff_attn_kernel, blk=blk, seq=seq, lam_init=lam_init),
        grid=(batch, heads),
        in_specs=in_specs,
        out_specs=pl.BlockSpec((seq, dh), lambda b, h: (b, h)),
        out_shape=jax.ShapeDtypeStruct((t, width), BF16),
        scratch_shapes=[stat(), stat(), accum(), stat(), stat(), accum(),
                        pltpu.VMEM((2, blk, blk), F32), pltpu.VMEM((2, blk, blk), F32)],
        compiler_params=_cparams(("parallel", "parallel"), 40),
        name="diff_attention",
    )(qv, k_t, qv, lam, subln_g.reshape(1, dh))


def _sb_qkv_epilogue(matmul, j, extra, outs, *, tn, width, scale):
    (wkt_ref,) = extra
    qv_ref, kt_ref = outs

    @pl.when(j * tn < width)
    def _():
        for c0, c1 in _column_parts(tn):
            qv_ref[:, c0:c1] = (matmul(c0, c1) * scale).astype(BF16)

    @pl.when(jnp.logical_and(j * tn >= width, j * tn < 2 * width))
    def _():
        kt_ref[...] = matmul.transposed(wkt_ref[...]).astype(BF16)

    @pl.when(j * tn >= 2 * width)
    def _():
        for c0, c1 in _column_parts(tn):
            qv_ref[:, c0:c1] = matmul(c0, c1).astype(BF16)


def _sb_mixer(x2d, mod, seq, batch, w_qkv):
    t, d = x2d.shape
    width = w_qkv.shape[1] // 3
    heads = width // HEAD_DIM
    scale = HEAD_DIM ** -0.5 * LOG2_E
    tm = min(512, seq)
    tn = min(1024, width)
    nq = width // tn
    w_bf16 = w_qkv.astype(BF16)
    w_kt = w_bf16[:, width:2 * width].T
    qv, k_t = _proj(
        x2d, w_bf16, tm=tm, tn=tn, mod=mod, seq=seq,
        extras=[(w_kt, (tn, d), lambda i, j: (jnp.clip(j - nq, 0, nq - 1), 0))],
        outs=_qkv_out_specs(t, width, tm, tn),
        epilogue=functools.partial(_sb_qkv_epilogue, tn=tn, width=width, scale=scale),
        name="sb_qkv")
    tq = min(512, seq)
    tk = min(256, seq)
    hg = ATTN_HEAD_GROUP
    groups = heads // hg
    return _attention(
        functools.partial(_sb_attn_kernel, tq=tq, tk=tk, seq=seq, heads=hg), qv, k_t, qv,
        batch=batch, seq=seq, groups=groups, k_off=0, v_off=groups,
        dq=hg * HEAD_DIM, dv=hg * HEAD_DIM, d_out=hg * HEAD_DIM,
        scratch=[pltpu.VMEM((hg, tq, LANES), F32), pltpu.VMEM((hg, tq, HEAD_DIM), F32)],
        vmem_mib=40, name="sb_attention")


def _split_bf16(v):
    hi = v.astype(BF16)
    return hi, (v - hi.astype(F32)).astype(BF16)


def _outproj_router_kernel(o_ref, w_ref, x_ref, mod_ref, g_ref, b_ref, rw_ref, rb_ref,
                           x1_ref, idx_ref, gate_ref, *, alpha, n_experts):
    y = jnp.dot(o_ref[...], w_ref[...], preferred_element_type=F32)
    mod = mod_ref[...]
    x1 = _layer_norm_rows(alpha * x_ref[...] + mod[2:3, :] * y, g_ref[...], b_ref[...])
    x1_ref[...] = x1
    h2 = x1 * (1.0 + mod[4:5, :]) + mod[3:4, :]
    hi, lo = _split_bf16(h2)
    logits = jnp.dot(jnp.concatenate([hi, lo, hi], axis=1), rw_ref[...],
                     preferred_element_type=F32) + rb_ref[...]
    lane = lax.broadcasted_iota(jnp.int32, logits.shape, 1).astype(F32)
    work = jnp.where(lane < n_experts, logits, -jnp.inf)
    idx_out = jnp.zeros(logits.shape, F32)
    val_out = jnp.zeros(logits.shape, F32)
    top = None
    denom = None
    for kk in range(TOP_K):
        m = jnp.max(work, axis=-1, keepdims=True)
        first = jnp.min(jnp.where(work == m, lane, float(LANES)), axis=-1, keepdims=True)
        if kk == 0:
            top = m
        e = jnp.exp(m - top)
        denom = e if kk == 0 else denom + e
        idx_out = jnp.where(lane == kk, first, idx_out)
        val_out = jnp.where(lane == kk, e, val_out)
        work = jnp.where(lane == first, -jnp.inf, work)
    idx_ref[...] = idx_out.astype(jnp.int32)
    gate_ref[...] = val_out / denom


def _outproj_router(o, w_o, x2d, mod, ln_g, ln_b, router_w, router_b, *, seq, alpha):
    t, d = x2d.shape
    n_experts = router_w.shape[1]
    tm = min(256, seq)
    blocks_per_seq = seq // tm
    once = pl.Buffered(1)
    rw = jnp.zeros((d, LANES), F32).at[:, :n_experts].set(router_w)
    rw_hi, rw_lo = _split_bf16(rw)
    rw3 = jnp.concatenate([rw_hi, rw_hi, rw_lo], axis=0)
    rb = jnp.zeros((1, LANES), F32).at[0, :n_experts].set(router_b)
    row = lambda i: (i, 0)
    fixed = lambda i: (0, 0)
    return pl.pallas_call(
        functools.partial(_outproj_router_kernel, alpha=alpha, n_experts=n_experts),
        grid=(t // tm,),
        in_specs=[
            pl.BlockSpec((tm, o.shape[1]), row),
            pl.BlockSpec(w_o.shape, fixed, pipeline_mode=once),
            pl.BlockSpec((tm, d), row),
            pl.BlockSpec((None, 6, d), lambda i: (i // blocks_per_seq, 0, 0)),
            pl.BlockSpec((1, d), fixed),
            pl.BlockSpec((1, d), fixed),
            pl.BlockSpec((3 * d, LANES), fixed, pipeline_mode=once),
            pl.BlockSpec((1, LANES), fixed),
        ],
        out_specs=[pl.BlockSpec((tm, d), row),
                   pl.BlockSpec((tm, LANES), row), pl.BlockSpec((tm, LANES), row)],
        out_shape=[jax.ShapeDtypeStruct((t, d), F32),
                   jax.ShapeDtypeStruct((t, LANES), jnp.int32), jax.ShapeDtypeStruct((t, LANES), F32)],
        compiler_params=_cparams(("parallel",), 48),
        name="outproj_ln_router",
    )(o, w_o.astype(BF16), x2d, mod, ln_g.reshape(1, d), ln_b.reshape(1, d), rw3, rb)


def _plan_kernel(idx_ref, rank_ref, cnt_ref, run_ref):
    @pl.when(pl.program_id(0) == 0)
    def _():
        run_ref[...] = jnp.zeros(run_ref.shape, F32)

    idx = idx_ref[...]
    tb = idx.shape[0]
    lane = lax.broadcasted_iota(jnp.int32, idx.shape, 1)
    onehots = [(lane == idx[:, kk:kk + 1]).astype(F32) for kk in range(TOP_K)]
    chosen = onehots[0]
    for kk in range(1, TOP_K):
        chosen = chosen + onehots[kk]
    earlier = (lax.broadcasted_iota(jnp.int32, (tb, tb), 1)
               < lax.broadcasted_iota(jnp.int32, (tb, tb), 0)).astype(BF16)
    prefix = jnp.dot(earlier, chosen.astype(BF16), preferred_element_type=F32) + run_ref[0:1, :]
    rank = jnp.zeros(idx.shape, F32)
    for kk in range(TOP_K):
        rank = jnp.where(lane == kk, jnp.sum(prefix * onehots[kk], axis=-1, keepdims=True), rank)
    rank_ref[...] = rank.astype(jnp.int32)
    run_ref[...] = run_ref[...] + jnp.sum(chosen, axis=0, keepdims=True)
    cnt_ref[...] = run_ref[...]


def _routing_plan(idx_pad, n_experts, tm):
    t = idx_pad.shape[0]
    tb = min(512, t)
    rank_pad, cnt = pl.pallas_call(
        _plan_kernel,
        grid=(t // tb,),
        in_specs=[pl.BlockSpec((tb, LANES), lambda i: (i, 0))],
        out_specs=[pl.BlockSpec((tb, LANES), lambda i: (i, 0)),
                   pl.BlockSpec((SUBLANES, LANES), lambda i: (0, 0))],
        out_shape=[jax.ShapeDtypeStruct((t, LANES), jnp.int32),
                   jax.ShapeDtypeStruct((SUBLANES, LANES), F32)],
        scratch_shapes=[pltpu.VMEM((SUBLANES, LANES), F32)],
        compiler_params=_cparams(("arbitrary",), 16),
        name="moe_plan",
    )(idx_pad)
    counts = cnt[0, :n_experts].astype(jnp.int32)
    padded = (counts + tm - 1) // tm * tm
    pad_end = jnp.cumsum(padded)
    pad_start = pad_end - padded
    chose = idx_pad[:, :TOP_K, None] == jnp.arange(n_experts, dtype=jnp.int32)
    dest = (jnp.sum(jnp.where(chose, pad_start, 0), axis=-1) + rank_pad[:, :TOP_K]).astype(jnp.int32)
    n_blocks = t * TOP_K // tm + n_experts
    block_start = jnp.arange(n_blocks, dtype=jnp.int32) * tm
    block_expert = jnp.minimum(jnp.sum(block_start[:, None] >= pad_end[None, :], axis=1),
                               n_experts - 1).astype(jnp.int32)
    n_used = (pad_end[-1] // tm).astype(jnp.int32).reshape(1)
    fill_start = jnp.concatenate([pad_start + counts, pad_end[-1:]]).astype(jnp.int32)
    fill_count = jnp.concatenate([pad_end - pad_start - counts,
                                  n_blocks * tm - pad_end[-1:]]).astype(jnp.int32)
    prev_expert = jnp.concatenate([jnp.full((1,), -1, jnp.int32), block_expert[:-1]])
    run_first = (block_expert != prev_expert).astype(jnp.int32)
    ids = jnp.arange(n_experts, dtype=jnp.int32)
    later_nonempty = jnp.logical_and(ids[None, :] > ids[:, None], counts[None, :] > 0)
    next_nonempty = jnp.min(jnp.where(later_nonempty, ids[None, :], n_experts), axis=1)
    run_next = jnp.where(next_nonempty < n_experts, next_nonempty, -1).astype(jnp.int32)[block_expert]
    return dict(dest=dest.reshape(-1), block_expert=block_expert, n_used=n_used, fill_start=fill_start,
                fill_count=fill_count, n_blocks=n_blocks, run_first=run_first, run_next=run_next)


def _row_pitch(chunks):
    return chunks + 1


def _pack_pairs(v):
    w = v.shape[1] // 2
    hi = lax.bitcast_convert_type(v[:, :w].astype(BF16).astype(F32), jnp.uint32)
    lo = lax.bitcast_convert_type(v[:, w:].astype(BF16).astype(F32), jnp.uint32)
    return hi | (lo >> 16)


def _unpack_pairs(u):
    hi = lax.bitcast_convert_type(u & jnp.uint32(0xFFFF0000), F32)
    lo = lax.bitcast_convert_type(u << 16, F32)
    return jnp.concatenate([hi, lo], axis=1)


def _load_rows(ref, lead, n_rows, chunks):
    pitch = _row_pitch(chunks)
    return jnp.concatenate(
        [ref[lead + (pl.ds(c, n_rows, stride=pitch), slice(None))] for c in range(chunks)], axis=1)


def _store_rows(ref, lead, val, chunks):
    pitch = _row_pitch(chunks)
    n_rows = val.shape[0]
    for c in range(chunks):
        ref[lead + (pl.ds(c, n_rows, stride=pitch), slice(None))] = val[:, c * LANES:(c + 1) * LANES]
    ref[lead + (pl.ds(chunks, n_rows, stride=pitch), slice(None))] = jnp.zeros((n_rows, LANES), val.dtype)


def _dispatch_kernel(dest_ref, fstart_ref, fcount_ref, x_ref, mod_ref, xs_hbm, hbuf, zrow, sem, fsem,
                     *, tb, n_steps, n_fills, chunks):
    i = pl.program_id(0)
    slot = i % 2
    pitch = _row_pitch(chunks)

    def drain(buf_slot):
        for _ in range(TOP_K):
            pltpu.make_async_copy(hbuf.at[buf_slot], xs_hbm.at[pl.ds(0, tb * pitch)],
                                  sem.at[buf_slot]).wait()

    def for_each_fill(fn):
        def per_range(f, c):
            start = fstart_ref[f]

            def per_row(j, c2):
                fn(pltpu.make_async_copy(zrow, xs_hbm.at[pl.ds((start + j) * pitch, pitch)], fsem))
                return c2

            lax.fori_loop(0, fcount_ref[f], per_row, 0)
            return c

        lax.fori_loop(0, n_fills, per_range, 0)

    @pl.when(i == 0)
    def _():
        zrow[...] = jnp.zeros(zrow.shape, zrow.dtype)
        for_each_fill(lambda copy: copy.start())

    @pl.when(i >= 2)
    def _():
        drain(slot)

    mod = mod_ref[...]
    h2 = x_ref[...] * (1.0 + mod[4:5, :]) + mod[3:4, :]
    _store_rows(hbuf, (slot,), _pack_pairs(h2), chunks)

    def per_token(r, c):
        base = (i * tb + r) * TOP_K
        src = hbuf.at[slot, pl.ds(r * pitch, pitch)]
        for kk in range(TOP_K):
            pltpu.make_async_copy(src, xs_hbm.at[pl.ds(dest_ref[base + kk] * pitch, pitch)],
                                  sem.at[slot]).start(priority=kk % 2)
        return c

    lax.fori_loop(0, tb, per_token, 0, unroll=4)

    @pl.when(i == n_steps - 1)
    def _():
        drain(slot)
        if n_steps > 1:
            drain(1 - slot)
        for_each_fill(lambda copy: copy.wait())


def _dispatch(x1, mod, plan, *, seq, tm):
    dest, fill_start, fill_count = plan["dest"], plan["fill_start"], plan["fill_count"]
    n_blocks = plan["n_blocks"]
    t, d = x1.shape
    chunks = d // (2 * LANES)
    pitch = _row_pitch(chunks)
    tb = min(256, seq)
    n_steps = t // tb
    blocks_per_seq = seq // tb
    grid_spec = pltpu.PrefetchScalarGridSpec(
        num_scalar_prefetch=3,
        grid=(n_steps,),
        in_specs=[
            pl.BlockSpec((tb, d), lambda i, *_: (i, 0)),
            pl.BlockSpec((None, 6, d), lambda i, *_: (i // blocks_per_seq, 0, 0)),
        ],
        out_specs=pl.BlockSpec(memory_space=pl.ANY),
        scratch_shapes=[pltpu.VMEM((2, tb * pitch, LANES), jnp.uint32),
                        pltpu.VMEM((pitch, LANES), jnp.uint32),
                        pltpu.SemaphoreType.DMA((2,)), pltpu.SemaphoreType.DMA(())],
    )
    return pl.pallas_call(
        functools.partial(_dispatch_kernel, tb=tb, n_steps=n_steps, n_fills=fill_start.shape[0],
                          chunks=chunks),
        grid_spec=grid_spec,
        out_shape=jax.ShapeDtypeStruct((n_blocks * tm * pitch, LANES), jnp.uint32),
        compiler_params=_cparams(("arbitrary",), 32),
        name="moe_dispatch",
    )(dest, fill_start, fill_count, x1, mod)


def _expert_kernel(be_ref, used_ref, first_ref, next_ref, xs_ref, wgu_hbm, bgu_ref, wd_hbm, bd_ref,
                   y_ref, gu_stage, dn_stage, gu_bf16, dn_bf16, sem, *, layer, tm, d_expert, chunks):
    i = pl.program_id(0)

    def weight_copies(expert):
        return (pltpu.make_async_copy(wgu_hbm.at[layer, expert], gu_stage, sem.at[0]),
                pltpu.make_async_copy(wd_hbm.at[layer, expert], dn_stage, sem.at[1]))

    def cast(stage, out):
        n_rows = stage.shape[0]
        slab = math.gcd(n_rows, 256)

        def body(j, c):
            rows = pl.ds(pl.multiple_of(j * slab, slab), slab)
            out[rows, :] = stage[rows, :].astype(BF16)
            return c
        lax.fori_loop(0, n_rows // slab, body, 0)

    @pl.when(i == 0)
    def _():
        for copy in weight_copies(be_ref[0]):
            copy.start()

    @pl.when(jnp.logical_and(i < used_ref[0], first_ref[i] == 1))
    def _():
        for copy in weight_copies(be_ref[i]):
            copy.wait()
        cast(gu_stage, gu_bf16)
        cast(dn_stage, dn_bf16)

        @pl.when(next_ref[i] >= 0)
        def _():
            for copy in weight_copies(next_ref[i]):
                copy.start()

    @pl.when(i < used_ref[0])
    def _():
        x = _unpack_pairs(_load_rows(xs_ref, (), tm, chunks)).astype(BF16)
        gu = jnp.dot(x, gu_bf16[...], preferred_element_type=F32) + bgu_ref[...]
        g = jnp.minimum(gu[:, :d_expert], SWIGLU_LIMIT)
        lin = jnp.clip(gu[:, d_expert:], -SWIGLU_LIMIT, SWIGLU_LIMIT)
        act = g * jax.nn.sigmoid(SWIGLU_ALPHA * g) * (lin + 1.0)
        y = jnp.dot(act.astype(BF16), dn_bf16[...], preferred_element_type=F32) + bd_ref[...]
        _store_rows(y_ref, (), _pack_pairs(y), chunks)

    @pl.when(i >= used_ref[0])
    def _():
        y_ref[...] = jnp.zeros(y_ref.shape, y_ref.dtype)


def _experts(xs, plan, layer, w_gate_up, b_gate_up, w_down, b_down, tm):
    n_blocks = plan["n_blocks"]
    depth, n_experts, d_expert, d = w_down.shape
    chunks = d // (2 * LANES)
    pitch = _row_pitch(chunks)
    per_expert = lambda i, be, used, first, nxt: (layer, be[i], 0, 0)
    rows = lambda i, be, used, first, nxt: (i, 0)
    used_rows = lambda i, be, used, first, nxt: (jnp.minimum(i, used[0] - 1), 0)
    grid_spec = pltpu.PrefetchScalarGridSpec(
        num_scalar_prefetch=4,
        grid=(n_blocks,),
        in_specs=[
            pl.BlockSpec((tm * pitch, LANES), used_rows),
            pl.BlockSpec(memory_space=pl.ANY),
            pl.BlockSpec((None, None, 1, 2 * d_expert), per_expert),
            pl.BlockSpec(memory_space=pl.ANY),
            pl.BlockSpec((None, None, 1, d), per_expert),
        ],
        out_specs=pl.BlockSpec((tm * pitch, LANES), rows),
        scratch_shapes=[pltpu.VMEM((d, 2 * d_expert), F32), pltpu.VMEM((d_expert, d), F32),
                        pltpu.VMEM((d, 2 * d_expert), BF16), pltpu.VMEM((d_expert, d), BF16),
                        pltpu.SemaphoreType.DMA((2,))],
    )
    return pl.pallas_call(
        functools.partial(_expert_kernel, layer=layer, tm=tm, d_expert=d_expert, chunks=chunks),
        grid_spec=grid_spec,
        out_shape=jax.ShapeDtypeStruct((n_blocks * tm * pitch, LANES), jnp.uint32),
        compiler_params=_cparams(("arbitrary",), 56),
        name="moe_experts",
    )(plan["block_expert"], plan["n_used"], plan["run_first"], plan["run_next"], xs, w_gate_up,
      b_gate_up.reshape(depth, n_experts, 1, 2 * d_expert), w_down,
      b_down.reshape(depth, n_experts, 1, d))


def _combine_kernel(dest_ref, ys_hbm, x_ref, gate_ref, mod_ref, g_ref, b_ref, o_ref, ybuf, sem,
                    *, tn, n_blocks, alpha, chunks):
    i = pl.program_id(0)
    slot = i % 2
    pitch = _row_pitch(chunks)

    def issue(block, buf_slot):
        def body(r, c):
            base = (block * tn + r) * TOP_K
            for kk in range(TOP_K):
                pltpu.make_async_copy(ys_hbm.at[pl.ds(dest_ref[base + kk] * pitch, pitch)],
                                      ybuf.at[buf_slot, kk, pl.ds(r * pitch, pitch)],
                                      sem.at[buf_slot]).start(priority=kk % 2)
            return c
        lax.fori_loop(0, tn, body, 0, unroll=4)

    @pl.when(i == 0)
    def _():
        issue(0, 0)

    @pl.when(i + 1 < n_blocks)
    def _():
        issue(i + 1, 1 - slot)

    for kk in range(TOP_K):
        pltpu.make_async_copy(ys_hbm.at[pl.ds(0, tn * pitch)], ybuf.at[slot, kk], sem.at[slot]).wait()

    gate = gate_ref[...]
    y = None
    for kk in range(TOP_K):
        term = gate[:, kk:kk + 1] * _unpack_pairs(_load_rows(ybuf, (slot, kk), tn, chunks))
        y = term if y is None else y + term
    mod = mod_ref[...]
    o_ref[...] = _layer_norm_rows(alpha * x_ref[...] + mod[5:6, :] * y, g_ref[...], b_ref[...])


def _combine(ys, dest, x1, gate_pad, mod, ln_g, ln_b, *, seq, alpha):
    t, d = x1.shape
    chunks = d // (2 * LANES)
    pitch = _row_pitch(chunks)
    tn = min(128, seq)
    n_blocks = t // tn
    blocks_per_seq = seq // tn
    grid_spec = pltpu.PrefetchScalarGridSpec(
        num_scalar_prefetch=1,
        grid=(n_blocks,),
        in_specs=[
            pl.BlockSpec(memory_space=pl.ANY),
            pl.BlockSpec((tn, d), lambda i, dst: (i, 0)),
            pl.BlockSpec((tn, LANES), lambda i, dst: (i, 0)),
            pl.BlockSpec((None, 6, d), lambda i, dst: (i // blocks_per_seq, 0, 0)),
            pl.BlockSpec((1, d), lambda i, dst: (0, 0)),
            pl.BlockSpec((1, d), lambda i, dst: (0, 0)),
        ],
        out_specs=pl.BlockSpec((tn, d), lambda i, dst: (i, 0)),
        scratch_shapes=[pltpu.VMEM((2, TOP_K, tn * pitch, LANES), jnp.uint32), pltpu.SemaphoreType.DMA((2,))],
    )
    return pl.pallas_call(
        functools.partial(_combine_kernel, tn=tn, n_blocks=n_blocks, alpha=alpha, chunks=chunks),
        grid_spec=grid_spec,
        out_shape=jax.ShapeDtypeStruct((t, d), F32),
        compiler_params=_cparams(("arbitrary",), 40),
        name="moe_combine_ln",
    )(dest, ys, x1, gate_pad, mod, ln_g.reshape(1, d), ln_b.reshape(1, d))


def kernel(x, c, ada_w, ada_b, ln_g, ln_b, mla_w_in, mla_q_norm_g, mla_kv_norm_g, mla_w_uq, mla_w_ukv,
           mla_w_o, diff_w_qkv, diff_lambda, diff_subln_g, diff_w_o, sb_w_qkv, sb_w_o, moe_router_w,
           moe_router_b, moe_w_gate_up, moe_b_gate_up, moe_w_down, moe_b_down):
    batch, seq, d = x.shape
    depth = ada_w.shape[0]
    alpha = (2 * depth) ** 0.25
    n_experts = moe_router_w.shape[-1]
    moe_tm = 256
    mods = _adaln(c, ada_w, ada_b)
    x2d = x.reshape(batch * seq, d)
    for i in range(depth):
        mod = mods[i]
        kind, j = i % N_MIXERS, i // N_MIXERS
        if kind == 0:
            o = _mla_mixer(x2d, mod, seq, batch, mla_w_in[j], mla_q_norm_g[j], mla_kv_norm_g[j],
                           mla_w_uq[j], mla_w_ukv[j])
            w_o = mla_w_o[j]
        elif kind == 1:
            o = _diff_mixer(x2d, mod, seq, batch, i, diff_w_qkv[j], diff_lambda[j], diff_subln_g[j])
            w_o = diff_w_o[j]
        else:
            o = _sb_mixer(x2d, mod, seq, batch, sb_w_qkv[j])
            w_o = sb_w_o[j]
        x1, idx_pad, gate_pad = _outproj_router(
            o, w_o, x2d, mod, ln_g[i, 0], ln_b[i, 0], moe_router_w[i], moe_router_b[i],
            seq=seq, alpha=alpha)
        plan = _routing_plan(idx_pad, n_experts, moe_tm)
        xs = _dispatch(x1, mod, plan, seq=seq, tm=moe_tm)
        ys = _experts(xs, plan, i, moe_w_gate_up, moe_b_gate_up, moe_w_down, moe_b_down, moe_tm)
        x2d = _combine(ys, plan["dest"], x1, gate_pad, mod, ln_g[i, 1], ln_b[i, 1], seq=seq,
                       alpha=alpha)
    return x2d.reshape(batch, seq, d)
```
